```python
import jax, jax.numpy as jnp
from jax import lax
import numpy as np

D_MODEL = 1024
BATCH = 4
SEQ = 8192
DEPTH = 1
DEC_BATCH = 16
DEC_SEQ = 16
PAST_LEN = 4096

CHUNK = 64
N_META = 16
CONV_WIDTH = 3
CONV_DIM = D_MODEL
SB_HEADS = 8
SB_HEAD_DIM = D_MODEL // SB_HEADS
SB_DIM = SB_HEADS * SB_HEAD_DIM
Q_BLOCK = 128
EPS = 1e-6
SPLITS = (CONV_DIM, CONV_DIM, CONV_DIM, CONV_DIM, SB_DIM, SB_DIM, SB_DIM, SB_DIM, D_MODEL, D_MODEL)
IN_DIM = sum(SPLITS)
SPLIT_OFFSETS = [int(o) for o in np.cumsum(SPLITS)[:-1]]

kernel_name = "gated_conv_stickbreak_hybrid_step"


def rmsnorm(x, g):
    xf = x.astype(jnp.float32)
    y = xf * lax.rsqrt(jnp.mean(xf * xf, axis=-1, keepdims=True) + EPS)
    return (y * g.astype(jnp.float32)).astype(x.dtype)


def in_projection(x, norm_g, w_in, b_in, q_norm_g, k_norm_g):
    bsz, t = x.shape[0], x.shape[1]
    xn = rmsnorm(x, norm_g)
    p = jnp.einsum('btd,de->bte', xn, w_in) + b_in
    b_c, c_c, h_c, z_c, q, k, v, z_s, g_c, g_s = jnp.split(p, SPLIT_OFFSETS, axis=-1)
    q = rmsnorm(q.reshape(bsz, t, SB_HEADS, SB_HEAD_DIM), q_norm_g)
    k = rmsnorm(k.reshape(bsz, t, SB_HEADS, SB_HEAD_DIM), k_norm_g)
    v = v.reshape(bsz, t, SB_HEADS, SB_HEAD_DIM)
    return b_c, c_c, h_c, z_c, q, k, v, z_s, g_c, g_s


def short_conv(u, prev, w):
    t = u.shape[1]
    full = jnp.concatenate([prev.astype(u.dtype), u], axis=1)
    y = w[0] * full[:, 0:t]
    for i in range(1, CONV_WIDTH):
        y = y + w[i] * full[:, i:i + t]
    return y, full[:, -(CONV_WIDTH - 1):]


def stick_breaking(q, k, v, q_pos, k_pos):
    scale = SB_HEAD_DIM ** -0.5
    z = jnp.einsum('bqhd,bkhd->bhqk', q, k).astype(jnp.float32) * scale
    mask = (k_pos[None, :] < q_pos[:, None])[None, None]
    log_beta = jax.nn.log_sigmoid(z)
    log_1m_beta = jnp.where(mask, jax.nn.log_sigmoid(-z), 0.0)
    suffix = lax.cumsum(log_1m_beta, axis=3, reverse=True) - log_1m_beta
    a = jnp.where(mask, jnp.exp(log_beta + suffix), 0.0)
    return jnp.einsum('bhqk,bkhd->bqhd', a.astype(v.dtype), v)


def merge_out(x, y_c, y_s, g_c, g_s, w_conv_o, w_sb_o, w_o):
    m = (jax.nn.sigmoid(g_c) * jnp.einsum('btc,cd->btd', y_c, w_conv_o)
         + jax.nn.sigmoid(g_s) * jnp.einsum('bte,ed->btd', y_s, w_sb_o))
    return x + jnp.einsum('btd,de->bte', m, w_o)


def prompt_layer(h, norm_g, w_in, b_in, q_norm_g, k_norm_g, conv_w, w_conv_o, w_sb_o, w_o):
    bsz, t = h.shape[0], h.shape[1]
    b_c, c_c, h_c, z_c, q, k, v, z_s, g_c, g_s = in_projection(h, norm_g, w_in, b_in, q_norm_g, k_norm_g)
    conv_out, conv_state = short_conv(c_c * h_c, jnp.zeros((bsz, CONV_WIDTH - 1, CONV_DIM), h.dtype), conv_w)
    y_c = jax.nn.silu(z_c) * (b_c * conv_out)
    nb = -(-t // Q_BLOCK)
    tp = nb * Q_BLOCK
    qp = jnp.pad(q, ((0, 0), (0, tp - t), (0, 0), (0, 0)))
    qb = qp.reshape(bsz, nb, Q_BLOCK, SB_HEADS, SB_HEAD_DIM).transpose(1, 0, 2, 3, 4)
    posb = jnp.arange(tp).reshape(nb, Q_BLOCK)
    k_pos = jnp.arange(t)
    ob = lax.map(lambda a: stick_breaking(a[0], k, v, a[1], k_pos), (qb, posb))
    o = ob.transpose(1, 0, 2, 3, 4).reshape(bsz, tp, SB_DIM)[:, :t]
    y_s = jax.nn.silu(z_s) * o
    y = merge_out(h, y_c, y_s, g_c, g_s, w_conv_o, w_sb_o, w_o)
    return y, k, v, conv_state


def sample_layer(h, k_past, v_past, conv_prev, norm_g, w_in, b_in, q_norm_g, k_norm_g, conv_w, w_conv_o, w_sb_o, w_o):
    bsz, s = h.shape[0], h.shape[1]
    p_len = k_past.shape[1]
    b_c, c_c, h_c, z_c, q, k, v, z_s, g_c, g_s = in_projection(h, norm_g, w_in, b_in, q_norm_g, k_norm_g)
    conv_out, conv_state = short_conv(c_c * h_c, conv_prev, conv_w)
    y_c = jax.nn.silu(z_c) * (b_c * conv_out)
    k_all = jnp.concatenate([k_past.astype(k.dtype), k], axis=1)
    v_all = jnp.concatenate([v_past.astype(v.dtype), v], axis=1)
    q_pos = p_len + jnp.arange(s)
    k_pos = jnp.arange(p_len + s)
    o = stick_breaking(q, k_all, v_all, q_pos, k_pos).reshape(bsz, s, SB_DIM)
    y_s = jax.nn.silu(z_s) * o
    y = merge_out(h, y_c, y_s, g_c, g_s, w_conv_o, w_sb_o, w_o)
    return y, k, v, conv_state


def setup_inputs(seed: int = 0) -> dict:
    key = jax.random.key(seed)
    ks = jax.random.split(key, 16)
    nrm = jax.random.normal
    f32 = jnp.float32
    return {
        "x_prompt": nrm(ks[0], (BATCH, SEQ, D_MODEL), f32),
        "x_sample": nrm(ks[1], (DEC_BATCH, DEC_SEQ, D_MODEL), f32),
        "cache_k": nrm(ks[2], (DEPTH, DEC_BATCH, PAST_LEN, SB_HEADS, SB_HEAD_DIM), f32),
        "cache_v": nrm(ks[3], (DEPTH, DEC_BATCH, PAST_LEN, SB_HEADS, SB_HEAD_DIM), f32),
        "state_conv": nrm(ks[4], (DEPTH, DEC_BATCH, CONV_WIDTH - 1, CONV_DIM), f32),
        "meta_tokens": nrm(ks[5], (N_META, D_MODEL), f32),
        "norm_g": 1.0 + 0.01 * nrm(ks[6], (DEPTH, D_MODEL), f32),
        "w_in": nrm(ks[7], (DEPTH, D_MODEL, IN_DIM), f32) * D_MODEL ** -0.5,
        "b_in": 0.01 * nrm(ks[8], (DEPTH, IN_DIM), f32),
        "q_norm_g": 1.0 + 0.01 * nrm(ks[9], (DEPTH, SB_HEAD_DIM), f32),
        "k_norm_g": 1.0 + 0.01 * nrm(ks[10], (DEPTH, SB_HEAD_DIM), f32),
        "conv_w": nrm(ks[11], (DEPTH, CONV_WIDTH, CONV_DIM), f32) * CONV_WIDTH ** -0.5,
        "w_conv_o": nrm(ks[12], (DEPTH, CONV_DIM, D_MODEL), f32) * CONV_DIM ** -0.5,
        "w_sb_o": nrm(ks[13], (DEPTH, SB_DIM, D_MODEL), f32) * SB_DIM ** -0.5,
        "w_o": nrm(ks[14], (DEPTH, D_MODEL, D_MODEL), f32) * D_MODEL ** -0.5,
    }


def reference(x_prompt, x_sample, cache_k, cache_v, state_conv, meta_tokens, norm_g, w_in, b_in,
              q_norm_g, k_norm_g, conv_w, w_conv_o, w_sb_o, w_o):
    meta = jnp.broadcast_to(meta_tokens.astype(x_prompt.dtype)[None], (x_prompt.shape[0], N_META, D_MODEL))
    hp = jnp.concatenate([meta, x_prompt], axis=1)
    hs = x_sample
    kp_list, vp_list, cp_list, ksl, vsl, csl = [], [], [], [], [], []
    for l in range(DEPTH):
        hp, kp, vp, cp = prompt_layer(hp, norm_g[l], w_in[l], b_in[l], q_norm_g[l], k_norm_g[l],
                                      conv_w[l], w_conv_o[l], w_sb_o[l], w_o[l])
        hs, ks_, vs_, cs_ = sample_layer(hs, cache_k[l], cache_v[l], state_conv[l], norm_g[l], w_in[l], b_in[l],
                                         q_norm_g[l], k_norm_g[l], conv_w[l], w_conv_o[l], w_sb_o[l], w_o[l])
        kp_list.append(kp); vp_list.append(vp); cp_list.append(cp)
        ksl.append(ks_); vsl.append(vs_); csl.append(cs_)
    y_prompt = hp[:, N_META:]
    y_sample = hs
    return (y_prompt, y_sample, jnp.stack(kp_list), jnp.stack(vp_list), jnp.stack(cp_list),
            jnp.stack(ksl), jnp.stack(vsl), jnp.stack(csl))
```

```python
import functools

import jax
import jax.numpy as jnp
from jax import lax
from jax.experimental import pallas as pl
from jax.experimental.pallas import tpu as pltpu

EPS = 1e-6
N_SPLITS = 10
V7X_VMEM_LIMIT_BYTES = 60000 * 1024
PROMPT_ROW_TILE = 256
ATTN_TILE = 256
SAMPLE_KEY_TILE = 512

F32 = jnp.float32
BF16 = jnp.bfloat16


def _softplus(s):
    return jnp.maximum(s, 0.0) + jnp.log1p(jnp.exp(-jnp.abs(s)))


def _split_bf16(x):
    hi = x.astype(BF16)
    lo = (x - hi.astype(F32)).astype(BF16)
    return hi, lo


def _proj_kernel(x_ref, prev_ref, ng_ref, win_ref, bin_ref, qg_ref, kg_ref, cw_ref, wco_ref,
                 q_ref, k_ref, v_ref, k32_ref, v32_ref, mc_ref, zs_ref, gs_ref, cs_ref,
                 carry_ref, *, n_heads):
    t = pl.program_id(1)
    bt, tt, d = x_ref.shape
    rows = bt * tt
    hd = d // n_heads

    x = x_ref[...].reshape(rows, d)
    xn = x * lax.rsqrt(jnp.mean(x * x, axis=-1, keepdims=True) + EPS) * ng_ref[...]
    xn = xn.astype(BF16)

    def proj(i):
        w = win_ref[:, i * d:(i + 1) * d]
        return jnp.dot(xn, w, preferred_element_type=F32) + bin_ref[:, i * d:(i + 1) * d]

    u = proj(1) * proj(2)

    @pl.when(t == 0)
    def _():
        carry_ref[...] = prev_ref[...]

    prev = carry_ref[...]
    p0 = prev[:, 0:1, :]
    p1 = prev[:, 1:2, :]
    u3 = u.reshape(bt, tt, d)
    tidx = lax.broadcasted_iota(jnp.int32, (bt, tt, d), 1)
    r1 = pltpu.roll(u, 1, 0).reshape(bt, tt, d)
    r2 = pltpu.roll(u, 2, 0).reshape(bt, tt, d)
    um1 = jnp.where(tidx == 0, p1, r1)
    um2 = jnp.where(tidx == 0, p0, jnp.where(tidx == 1, p1, r2))
    cw = cw_ref[...]
    conv = cw[0:1, :] * um2 + cw[1:2, :] * um1 + cw[2:3, :] * u3
    last = u3[:, tt - 2:tt, :]
    carry_ref[...] = last
    cs_ref[...] = last

    yc = jax.nn.silu(proj(3)) * (proj(0) * conv.reshape(rows, d))
    yco = jnp.dot(yc.astype(BF16), wco_ref[...], preferred_element_type=F32)
    mc_ref[...] = (jax.nn.sigmoid(proj(8)) * yco).reshape(bt, tt, d)

    def head_norm(p, g_ref, scale):
        outs = []
        for h in range(n_heads):
            ph = p[:, h * hd:(h + 1) * hd]
            n = ph * lax.rsqrt(jnp.mean(ph * ph, axis=-1, keepdims=True) + EPS) * g_ref[...]
            outs.append(n * scale if scale is not None else n)
        return jnp.concatenate(outs, axis=-1)

    qn = head_norm(proj(4), qg_ref, hd ** -0.5)
    q_ref[...] = qn.astype(BF16).reshape(bt, tt, d)
    kn = head_norm(proj(5), kg_ref, None)
    k32_ref[...] = kn.reshape(bt, tt, d)
    k_ref[...] = kn.astype(BF16).reshape(bt, tt, d)
    vv = proj(6)
    v32_ref[...] = vv.reshape(bt, tt, d)
    v_ref[...] = vv.astype(BF16).reshape(bt, tt, d)
    zs_ref[...] = jax.nn.silu(proj(7)).reshape(bt, tt, d)
    gs_ref[...] = jax.nn.sigmoid(proj(9)).reshape(bt, tt, d)


def _proj(x, prev, ng, win, bin_, qg, kg, cw, wco, *, block_b, block_t, n_heads):
    b, t, d = x.shape
    grid = (b // block_b, t // block_t)
    row_spec = pl.BlockSpec((block_b, block_t, d), lambda i, j: (i, j, 0))
    state_spec = pl.BlockSpec((block_b, 2, d), lambda i, j: (i, 0, 0))

    def const(shape):
        return pl.BlockSpec(shape, lambda i, j: (0,) * len(shape), pipeline_mode=pl.Buffered(1))

    bf = jax.ShapeDtypeStruct((b, t, d), BF16)
    f32 = jax.ShapeDtypeStruct((b, t, d), F32)
    outs = pl.pallas_call(
        functools.partial(_proj_kernel, n_heads=n_heads),
        grid=grid,
        in_specs=[row_spec, state_spec, const((1, d)), const(win.shape), const(bin_.shape),
                  const(qg.shape), const(kg.shape), const(cw.shape), const(wco.shape)],
        out_specs=[row_spec] * 8 + [state_spec],
        out_shape=[bf, bf, bf, f32, f32, f32, f32, f32, jax.ShapeDtypeStruct((b, 2, d), F32)],
        scratch_shapes=[pltpu.VMEM((block_b, 2, d), F32)],
        compiler_params=pltpu.CompilerParams(
            dimension_semantics=("arbitrary", "arbitrary"),
            vmem_limit_bytes=V7X_VMEM_LIMIT_BYTES),
        name="proj",
    )(x, prev, ng, win, bin_, qg, kg, cw, wco)
    names = ("q", "k", "v", "k32", "v32", "mc", "zs", "gs", "cs")
    return dict(zip(names, outs))


def _merge_out(o, zs, gs, mc, x, wso_ref, wo_ref):
    ys = (zs * o).astype(BF16)
    m = mc + gs * jnp.dot(ys, wso_ref[...], preferred_element_type=F32)
    return x + jnp.dot(m.astype(BF16), wo_ref[...], preferred_element_type=F32)


def _attn_prompt_kernel(q_ref, k_ref, v_ref, km_ref, vm_ref, tri_ref, zs_ref, gs_ref, mc_ref, x_ref,
                        wso_ref, wo_ref, y_ref, acc_ref, r_ref, *, n_heads, n_meta):
    i = pl.program_id(1)
    tq = q_ref.shape[1]
    d = q_ref.shape[2]
    hd = d // n_heads

    acc_ref[...] = jnp.zeros_like(acc_ref)
    r_ref[...] = jnp.zeros_like(r_ref)

    def block(kblk, vblk, tri, mask):
        for h in range(n_heads):
            hs = slice(h * hd, (h + 1) * hd)
            s = lax.dot_general(q_ref[0, :, hs], kblk[:, hs], (((1,), (1,)), ((), ())),
                                preferred_element_type=F32)
            sp = _softplus(s)
            lm = -sp if mask is None else jnp.where(mask, -sp, 0.0)
            hi, lo = _split_bf16(lm)
            suf = (jnp.dot(hi, tri, preferred_element_type=F32)
                   + jnp.dot(lo, tri, preferred_element_type=F32))
            r = r_ref[h]
            a = jnp.exp((s - sp) + (suf + r))
            if mask is not None:
                a = jnp.where(mask, a, 0.0)
            acc_ref[h] += jnp.dot(a.astype(BF16), vblk[:, hs], preferred_element_type=F32)
            r_ref[h] = r + jnp.sum(lm, axis=-1, keepdims=True)

    row = lax.broadcasted_iota(jnp.int32, (tq, tq), 0)
    col = lax.broadcasted_iota(jnp.int32, (tq, tq), 1)
    start = pl.multiple_of(i * tq, tq)
    block(k_ref[0, pl.ds(start, tq), :], v_ref[0, pl.ds(start, tq), :], tri_ref[...], col < row)

    def body(j, c):
        st = pl.multiple_of((i - 1 - j) * tq, tq)
        block(k_ref[0, pl.ds(st, tq), :], v_ref[0, pl.ds(st, tq), :], tri_ref[...], None)
        return c

    lax.fori_loop(0, i, body, 0)

    wm = km_ref.shape[0]
    mcol = lax.broadcasted_iota(jnp.int32, (tq, wm), 1)
    block(km_ref[...], vm_ref[...], tri_ref[0:wm, 0:wm], mcol < n_meta)

    o = jnp.concatenate([acc_ref[h] for h in range(n_heads)], axis=-1)
    y_ref[0] = _merge_out(o, zs_ref[0], gs_ref[0], mc_ref[0], x_ref[0], wso_ref, wo_ref)


def _attn_prompt(q, k, v, km, vm, tri, zs, gs, mc, x, wso, wo, *, n_heads, n_meta):
    b, t, d = q.shape
    tq = ATTN_TILE
    hd = d // n_heads
    tile = pl.BlockSpec((1, tq, d), lambda i, j: (i, j, 0))
    stream = pl.BlockSpec((1, t, d), lambda i, j: (i, 0, 0), pipeline_mode=pl.Buffered(1))

    def const(shape):
        return pl.BlockSpec(shape, lambda i, j: (0,) * len(shape), pipeline_mode=pl.Buffered(1))

    return pl.pallas_call(
        functools.partial(_attn_prompt_kernel, n_heads=n_heads, n_meta=n_meta),
        grid=(b, t // tq),
        in_specs=[tile, stream, stream, const(km.shape), const(vm.shape), const(tri.shape),
                  tile, tile, tile, tile, const(wso.shape), const(wo.shape)],
        out_specs=tile,
        out_shape=jax.ShapeDtypeStruct((b, t, d), F32),
        scratch_shapes=[pltpu.VMEM((n_heads, tq, hd), F32), pltpu.VMEM((n_heads, tq, 1), F32)],
        compiler_params=pltpu.CompilerParams(
            dimension_semantics=("arbitrary", "arbitrary"),
            vmem_limit_bytes=V7X_VMEM_LIMIT_BYTES),
        name="attn_prompt",
    )(q, k, v, km, vm, tri, zs, gs, mc, x, wso, wo)


def _attn_sample_kernel(qbd_ref, kn_ref, vn_ref, ck_ref, cv_ref, tri_ref, zs_ref, gs_ref, mc_ref, x_ref,
                        wso_ref, wo_ref, y_ref, acc_ref, r_ref, *, n_heads):
    c = pl.program_id(1)
    nc = pl.num_programs(1)
    s_new = kn_ref.shape[1]
    d = kn_ref.shape[2]
    hd = d // n_heads
    lanes = qbd_ref.shape[2]

    def block(kblk, vblk, tri, mask):
        s = jnp.dot(kblk, qbd_ref[0], preferred_element_type=F32)
        sp = _softplus(s)
        lm = -sp if mask is None else jnp.where(mask, -sp, 0.0)
        hi, lo = _split_bf16(lm)
        suf = (jnp.dot(tri, hi, preferred_element_type=F32)
               + jnp.dot(tri, lo, preferred_element_type=F32))
        r = r_ref[...]
        a = jnp.exp((s - sp) + (suf + r))
        if mask is not None:
            a = jnp.where(mask, a, 0.0)
        acc_ref[...] += lax.dot_general(a.astype(BF16), vblk, (((0,), (0,)), ((), ())),
                                        preferred_element_type=F32)
        r_ref[...] = r + jnp.sum(lm, axis=0, keepdims=True)

    @pl.when(c == 0)
    def _():
        acc_ref[...] = jnp.zeros_like(acc_ref)
        r_ref[...] = jnp.zeros_like(r_ref)
        key = lax.broadcasted_iota(jnp.int32, (s_new, lanes), 0)
        qi = lax.broadcasted_iota(jnp.int32, (s_new, lanes), 1) % s_new
        block(kn_ref[0], vn_ref[0], tri_ref[0:s_new, 0:s_new], key < qi)

    block(ck_ref[0].astype(BF16), cv_ref[0].astype(BF16), tri_ref[...], None)

    @pl.when(c == nc - 1)
    def _():
        acc = acc_ref[...]
        o = jnp.concatenate(
            [acc[h * s_new:(h + 1) * s_new, h * hd:(h + 1) * hd] for h in range(n_heads)], axis=-1)
        y_ref[0] = _merge_out(o, zs_ref[0], gs_ref[0], mc_ref[0], x_ref[0], wso_ref, wo_ref)


def _attn_sample(qbd, kn, vn, ck, cv, tri, zs, gs, mc, x, wso, wo, *, n_heads):
    b, s_new, d = kn.shape
    p_len = ck.shape[1]
    tk = SAMPLE_KEY_TILE
    nc = p_len // tk
    lanes = qbd.shape[2]
    per_b = lambda shape: pl.BlockSpec((1,) + shape, lambda i, c: (i, 0, 0))
    cache = pl.BlockSpec((1, tk, d), lambda i, c: (i, nc - 1 - c, 0))

    def const(shape):
        return pl.BlockSpec(shape, lambda i, c: (0,) * len(shape), pipeline_mode=pl.Buffered(1))

    return pl.pallas_call(
        functools.partial(_attn_sample_kernel, n_heads=n_heads),
        grid=(b, nc),
        in_specs=[per_b((d, lanes)), per_b((s_new, d)), per_b((s_new, d)), cache, cache,
                  const(tri.shape), per_b((s_new, d)), per_b((s_new, d)), per_b((s_new, d)),
                  per_b((s_new, d)), const(wso.shape), const(wo.shape)],
        out_specs=per_b((s_new, d)),
        out_shape=jax.ShapeDtypeStruct((b, s_new, d), F32),
        scratch_shapes=[pltpu.VMEM((lanes, d), F32), pltpu.VMEM((1, lanes), F32)],
        compiler_params=pltpu.CompilerParams(
            dimension_semantics=("arbitrary", "arbitrary"),
            vmem_limit_bytes=V7X_VMEM_LIMIT_BYTES),
        name="attn_sample",
    )(qbd, kn, vn, ck, cv, tri, zs, gs, mc, x, wso, wo)


def kernel(x_prompt, x_sample, cache_k, cache_v, state_conv, meta_tokens, norm_g, w_in, b_in,
           q_norm_g, k_norm_g, conv_w, w_conv_o, w_sb_o, w_o):
    depth, dec_b, p_len, n_heads, hd = cache_k.shape
    assert depth == 1, "single-layer stack only"
    b, t, d = x_prompt.shape
    s_new = x_sample.shape[1]
    n_meta = meta_tokens.shape[0]
    assert n_heads * hd == d and conv_w.shape[1] == 3 and w_in.shape[2] == N_SPLITS * d
    assert n_meta == s_new, "meta tokens and running streams share one projection call"
    assert t % ATTN_TILE == 0 and t % PROMPT_ROW_TILE == 0 and p_len % SAMPLE_KEY_TILE == 0

    weights = (norm_g[0][None], w_in[0].astype(BF16), b_in[0][None], q_norm_g[0][None],
               k_norm_g[0][None], conv_w[0], w_conv_o[0].astype(BF16))
    wso = w_sb_o[0].astype(BF16)
    wo = w_o[0].astype(BF16)

    x_ms = jnp.concatenate([meta_tokens[None], x_sample], axis=0)
    prev_ms = jnp.concatenate([jnp.zeros((1, 2, d), F32), state_conv[0]], axis=0)
    ms = _proj(x_ms, prev_ms, *weights, block_b=dec_b + 1, block_t=s_new, n_heads=n_heads)

    prev_p = jnp.broadcast_to(ms["cs"][0:1], (b, 2, d))
    pr = _proj(x_prompt, prev_p, *weights, block_b=1, block_t=PROMPT_ROW_TILE, n_heads=n_heads)

    idx = jnp.arange(ATTN_TILE)
    tri_q = (idx[:, None] > idx[None, :]).astype(BF16)
    meta_pad = 128
    km = jnp.zeros((meta_pad, d), BF16).at[:n_meta].set(ms["k"][0])
    vm = jnp.zeros((meta_pad, d), BF16).at[:n_meta].set(ms["v"][0])
    y_prompt = _attn_prompt(pr["q"], pr["k"], pr["v"], km, vm, tri_q, pr["zs"], pr["gs"], pr["mc"],
                            x_prompt, wso, wo, n_heads=n_heads, n_meta=n_meta)

    idk = jnp.arange(SAMPLE_KEY_TILE)
    tri_k = (idk[None, :] > idk[:, None]).astype(BF16)
    qs = ms["q"][1:].reshape(dec_b, s_new, n_heads, hd)
    eye = jnp.eye(n_heads, dtype=BF16)
    qbd = jnp.einsum("bihd,hg->bhdgi", qs, eye).reshape(dec_b, d, n_heads * s_new)
    ck = cache_k[0].reshape(dec_b, p_len, d)
    cv = cache_v[0].reshape(dec_b, p_len, d)
    y_sample = _attn_sample(qbd, ms["k"][1:], ms["v"][1:], ck, cv, tri_k, ms["zs"][1:], ms["gs"][1:],
                            ms["mc"][1:], x_sample, wso, wo, n_heads=n_heads)

    def heads(a):
        return a.reshape(1, a.shape[0], a.shape[1], n_heads, hd)

    k_prompt = jnp.concatenate([jnp.broadcast_to(ms["k32"][0:1], (b, n_meta, d)), pr["k32"]], axis=1)
    v_prompt = jnp.concatenate([jnp.broadcast_to(ms["v32"][0:1], (b, n_meta, d)), pr["v32"]], axis=1)
    return (y_prompt, y_sample, heads(k_prompt), heads(v_prompt), pr["cs"][None],
            heads(ms["k32"][1:]), heads(ms["v32"][1:]), ms["cs"][1:][None])
```

```python
import functools
import math

import jax
import jax.numpy as jnp
from jax import lax
from jax.experimental import pallas as pl
from jax.experimental.pallas import tpu as pltpu

EPS = 1e-6
N_SPLITS = 10
UNDERFLOW_BITS = 160.0
EXP2_CLAMP = 64.0
LOG2_E = math.log2(math.e)
V7X_VMEM_LIMIT_BYTES = 60000 * 1024
V7X_SUBLANES = 8
PROMPT_ROW_TILE = 256
ATTN_TILE = 256
SAMPLE_KEY_TILE = 256

F32 = jnp.float32
BF16 = jnp.bfloat16


def _softplus2(z2):
    return jnp.maximum(z2, jnp.log(1.0 + jnp.exp2(jnp.minimum(z2, EXP2_CLAMP))) * LOG2_E)


def _proj_kernel(*refs, n_heads, n_lead):
    lead_refs, refs = (refs[:2], refs[2:]) if n_lead else ((), refs)
    (x_ref, prev_ref, ng_ref, win_ref, bin_ref, qg_ref, kg_ref, cw_ref, wco_ref,
     q_ref, k_ref, v_ref, k32_hbm, v32_hbm, mc_ref, zs_ref, gs_ref, cs_ref,
     carry_ref, kst_ref, vst_ref, sems) = refs
    bi = pl.program_id(0)
    t = pl.program_id(1)
    nt = pl.num_programs(1)
    step = bi * nt + t
    n_steps = pl.num_programs(0) * nt
    slot = step & 1
    bt, tt, d = x_ref.shape
    rows = bt * tt
    hd = d // n_heads

    def slab_copies(sl):
        r0 = pl.multiple_of((n_lead + t * tt) * n_heads, V7X_SUBLANES)
        return [pltpu.make_async_copy(
            stage_ref.at[sl], out_hbm.at[pl.ds(bi * bt, bt), pl.ds(r0, tt * n_heads), :], sems.at[i, sl])
            for i, (stage_ref, out_hbm) in enumerate(((kst_ref, k32_hbm), (vst_ref, v32_hbm)))]

    def stage(val, stage_ref):
        for h in range(n_heads):
            stage_ref[slot, :, pl.ds(h, tt, stride=n_heads), :] = (
                val[:, h * hd:(h + 1) * hd].reshape(bt, tt, hd))

    @pl.when(step >= 2)
    def _():
        for copy in slab_copies(slot):
            copy.wait()

    if n_lead:
        @pl.when(t == 0)
        def _():
            for lead_ref, out_hbm in zip(lead_refs, (k32_hbm, v32_hbm)):
                dst = out_hbm.at[pl.ds(bi * bt, bt), pl.ds(0, n_lead * n_heads), :]
                copy = pltpu.make_async_copy(lead_ref, dst, sems.at[2, 0])
                copy.start()
                copy.wait()

    x = x_ref[...].reshape(rows, d)
    xn = x * lax.rsqrt(jnp.mean(x * x, axis=-1, keepdims=True) + EPS) * ng_ref[...]
    xn = xn.astype(BF16)

    def proj(i):
        w = win_ref[:, i * d:(i + 1) * d]
        return jnp.dot(xn, w, preferred_element_type=F32) + bin_ref[:, i * d:(i + 1) * d]

    u = proj(1) * proj(2)

    @pl.when(t == 0)
    def _():
        carry_ref[...] = prev_ref[...]

    prev = carry_ref[...]
    p0 = prev[:, 0:1, :]
    p1 = prev[:, 1:2, :]
    u3 = u.reshape(bt, tt, d)
    tidx = lax.broadcasted_iota(jnp.int32, (bt, tt, d), 1)
    r1 = pltpu.roll(u, 1, 0).reshape(bt, tt, d)
    r2 = pltpu.roll(u, 2, 0).reshape(bt, tt, d)
    um1 = jnp.where(tidx == 0, p1, r1)
    um2 = jnp.where(tidx == 0, p0, jnp.where(tidx == 1, p1, r2))
    cw = cw_ref[...]
    conv = cw[0:1, :] * um2 + cw[1:2, :] * um1 + cw[2:3, :] * u3
    tail = u3[:, tt - 2:tt, :]
    carry_ref[...] = tail
    cs_ref[...] = tail

    yc = jax.nn.silu(proj(3)) * (proj(0) * conv.reshape(rows, d))
    yco = jnp.dot(yc.astype(BF16), wco_ref[...], preferred_element_type=F32)
    mc_ref[...] = (jax.nn.sigmoid(proj(8)) * yco).reshape(bt, tt, d)

    def head_norm(p, g_ref, scale):
        outs = []
        for h in range(n_heads):
            ph = p[:, h * hd:(h + 1) * hd]
            n = ph * lax.rsqrt(jnp.mean(ph * ph, axis=-1, keepdims=True) + EPS) * g_ref[...]
            outs.append(n * scale if scale is not None else n)
        return jnp.concatenate(outs, axis=-1)

    qn = head_norm(proj(4), qg_ref, LOG2_E * hd ** -0.5)
    q_ref[...] = qn.astype(BF16).reshape(bt, tt, d)
    kn = head_norm(proj(5), kg_ref, None)
    stage(kn, kst_ref)
    k_ref[...] = kn.astype(BF16).reshape(bt, tt, d)
    vv = proj(6)
    stage(vv, vst_ref)
    v_ref[...] = vv.astype(BF16).reshape(bt, tt, d)
    zs_ref[...] = jax.nn.silu(proj(7)).reshape(bt, tt, d)
    gs_ref[...] = jax.nn.sigmoid(proj(9)).reshape(bt, tt, d)

    for copy in slab_copies(slot):
        copy.start()

    @pl.when(step == n_steps - 1)
    def _():
        for copy in slab_copies(slot):
            copy.wait()

    @pl.when(jnp.logical_and(step == n_steps - 1, n_steps >= 2))
    def _():
        for copy in slab_copies(1 - slot):
            copy.wait()


def _proj(x, prev, ng, win, bin_, qg, kg, cw, wco, *, block_b, block_t, n_heads, lead_kv=()):
    b, t, d = x.shape
    hd = d // n_heads
    n_lead = lead_kv[0].shape[1] // n_heads if lead_kv else 0
    assert not lead_kv or block_b == 1
    grid = (b // block_b, t // block_t)
    row_spec = pl.BlockSpec((block_b, block_t, d), lambda i, j: (i, j, 0))
    state_spec = pl.BlockSpec((block_b, 2, d), lambda i, j: (i, 0, 0))
    hbm = pl.BlockSpec(memory_space=pl.ANY)

    def const(shape):
        return pl.BlockSpec(shape, lambda i, j: (0,) * len(shape), pipeline_mode=pl.Buffered(1))

    bf = jax.ShapeDtypeStruct((b, t, d), BF16)
    f32 = jax.ShapeDtypeStruct((b, t, d), F32)
    kv32 = jax.ShapeDtypeStruct((b, (n_lead + t) * n_heads, hd), F32)
    stage = pltpu.VMEM((2, block_b, block_t * n_heads, hd), F32)
    outs = pl.pallas_call(
        functools.partial(_proj_kernel, n_heads=n_heads, n_lead=n_lead),
        grid=grid,
        in_specs=[const(a.shape) for a in lead_kv]
        + [row_spec, state_spec, const((1, d)), const(win.shape), const(bin_.shape),
           const(qg.shape), const(kg.shape), const(cw.shape), const(wco.shape)],
        out_specs=[row_spec] * 3 + [hbm, hbm] + [row_spec] * 3 + [state_spec],
        out_shape=[bf, bf, bf, kv32, kv32, f32, f32, f32, jax.ShapeDtypeStruct((b, 2, d), F32)],
        scratch_shapes=[pltpu.VMEM((block_b, 2, d), F32), stage, stage, pltpu.SemaphoreType.DMA((3, 2))],
        compiler_params=pltpu.CompilerParams(
            dimension_semantics=("arbitrary", "arbitrary"),
            vmem_limit_bytes=V7X_VMEM_LIMIT_BYTES),
        name="proj",
    )(*lead_kv, x, prev, ng, win, bin_, qg, kg, cw, wco)
    names = ("q", "k", "v", "k32", "v32", "mc", "zs", "gs", "cs")
    return dict(zip(names, outs))


def _merge_out(o, zs, gs, mc, x, wso_ref, wo_ref):
    ys = (zs * o).astype(BF16)
    m = mc + gs * jnp.dot(ys, wso_ref[...], preferred_element_type=F32)
    return x + jnp.dot(m.astype(BF16), wo_ref[...], preferred_element_type=F32)


def _attn_prompt_kernel(q_ref, k_ref, v_ref, km_ref, vm_ref, tri_ref, zs_ref, gs_ref, mc_ref, x_ref,
                        wso_ref, wo_ref, y_ref, acc_ref, r_ref, *, n_heads, n_meta):
    i = pl.program_id(1)
    tq = q_ref.shape[1]
    d = q_ref.shape[2]
    hd = d // n_heads

    acc_ref[...] = jnp.zeros_like(acc_ref)
    r_ref[...] = jnp.zeros_like(r_ref)

    def block(kblk, vblk, tri, mask):
        logb, spms = [], []
        for h in range(n_heads):
            hs = slice(h * hd, (h + 1) * hd)
            z2 = lax.dot_general(q_ref[0, :, hs], kblk[:, hs], (((1,), (1,)), ((), ())),
                                 preferred_element_type=F32)
            sp = _softplus2(z2)
            logb.append(z2 - sp)
            spms.append(sp if mask is None else jnp.where(mask, sp, 0.0))
        suf_all = jnp.dot(jnp.concatenate([s.astype(BF16) for s in spms], axis=0), tri,
                          preferred_element_type=F32)
        r_low = None
        for h in range(n_heads):
            hs = slice(h * hd, (h + 1) * hd)
            r = r_ref[h]
            a = jnp.exp2(logb[h] - (suf_all[h * tq:(h + 1) * tq] + r))
            if mask is not None:
                a = jnp.where(mask, a, 0.0)
            acc_ref[h] += jnp.dot(a.astype(BF16), vblk[:, hs], preferred_element_type=F32)
            r_new = r + jnp.sum(spms[h], axis=-1, keepdims=True)
            r_ref[h] = r_new
            r_low = r_new if r_low is None else jnp.minimum(r_low, r_new)
        return jnp.min(r_low)

    row = lax.broadcasted_iota(jnp.int32, (tq, tq), 0)
    col = lax.broadcasted_iota(jnp.int32, (tq, tq), 1)
    start = pl.multiple_of(i * tq, tq)
    rmin = block(k_ref[0, pl.ds(start, tq), :], v_ref[0, pl.ds(start, tq), :], tri_ref[...], col < row)

    def cond(c):
        j, rmin = c
        return jnp.logical_and(j < i, rmin < UNDERFLOW_BITS)

    def body(c):
        j, _ = c
        st = pl.multiple_of((i - 1 - j) * tq, tq)
        rmin = block(k_ref[0, pl.ds(st, tq), :], v_ref[0, pl.ds(st, tq), :], tri_ref[...], None)
        return j + 1, rmin

    _, rmin = lax.while_loop(cond, body, (jnp.int32(0), rmin))

    @pl.when(rmin < UNDERFLOW_BITS)
    def _():
        wm = km_ref.shape[0]
        mcol = lax.broadcasted_iota(jnp.int32, (tq, wm), 1)
        block(km_ref[...], vm_ref[...], tri_ref[0:wm, 0:wm], mcol < n_meta)

    o = jnp.concatenate([acc_ref[h] for h in range(n_heads)], axis=-1)
    y_ref[0] = _merge_out(o, zs_ref[0], gs_ref[0], mc_ref[0], x_ref[0], wso_ref, wo_ref)


def _attn_prompt(q, k, v, km, vm, tri, zs, gs, mc, x, wso, wo, *, n_heads, n_meta):
    b, t, d = q.shape
    tq = ATTN_TILE
    hd = d // n_heads
    tile = pl.BlockSpec((1, tq, d), lambda i, j: (i, j, 0))
    stream = pl.BlockSpec((1, t, d), lambda i, j: (i, 0, 0), pipeline_mode=pl.Buffered(1))

    def const(shape):
        return pl.BlockSpec(shape, lambda i, j: (0,) * len(shape), pipeline_mode=pl.Buffered(1))

    return pl.pallas_call(
        functools.partial(_attn_prompt_kernel, n_heads=n_heads, n_meta=n_meta),
        grid=(b, t // tq),
        in_specs=[tile, stream, stream, const(km.shape), const(vm.shape), const(tri.shape),
                  tile, tile, tile, tile, const(wso.shape), const(wo.shape)],
        out_specs=tile,
        out_shape=jax.ShapeDtypeStruct((b, t, d), F32),
        scratch_shapes=[pltpu.VMEM((n_heads, tq, hd), F32), pltpu.VMEM((n_heads, tq, 1), F32)],
        compiler_params=pltpu.CompilerParams(
            dimension_semantics=("arbitrary", "arbitrary"),
            vmem_limit_bytes=V7X_VMEM_LIMIT_BYTES),
        name="attn_prompt",
    )(q, k, v, km, vm, tri, zs, gs, mc, x, wso, wo)


def _attn_sample_kernel(qbd_ref, kn_ref, vn_ref, ck_hbm, cv_hbm, tri_ref, zs_ref, gs_ref, mc_ref, x_ref,
                        wso_ref, wo_ref, y_ref, kbuf, vbuf, sems, acc_ref, r_ref, o_ref,
                        *, n_heads, tk):
    b = pl.program_id(0)
    nb = pl.num_programs(0)
    s_new = kn_ref.shape[1]
    d = kn_ref.shape[2]
    hd = d // n_heads
    lanes = qbd_ref.shape[2]
    p_len = ck_hbm.shape[1] // n_heads
    nc = p_len // tk

    def chunk_copies(bb, c, slot):
        r0 = pl.multiple_of((p_len - (c + 1) * tk) * n_heads, V7X_SUBLANES)
        rows = pl.ds(r0, tk * n_heads)
        return (pltpu.make_async_copy(ck_hbm.at[bb, rows, :], kbuf.at[slot], sems.at[0, slot]),
                pltpu.make_async_copy(cv_hbm.at[bb, rows, :], vbuf.at[slot], sems.at[1, slot]))

    def start_chunk(bb, c, slot):
        for cp in chunk_copies(bb, c, slot):
            cp.start()

    def wait_chunk(bb, c, slot):
        for cp in chunk_copies(bb, c, slot):
            cp.wait()

    @pl.when(b == 0)
    def _():
        start_chunk(b, 0, 0)

    def block(kblk, vblk, tri, mask):
        z2 = jnp.dot(kblk, qbd_ref[0], preferred_element_type=F32)
        sp = _softplus2(z2)
        spm = sp if mask is None else jnp.where(mask, sp, 0.0)
        suf = jnp.dot(tri, spm.astype(BF16), preferred_element_type=F32)
        r = r_ref[...]
        a = jnp.exp2((z2 - sp) - (suf + r))
        if mask is not None:
            a = jnp.where(mask, a, 0.0)
        acc_ref[...] += lax.dot_general(a.astype(BF16), vblk, (((0,), (0,)), ((), ())),
                                        preferred_element_type=F32)
        r_new = r + jnp.sum(spm, axis=0, keepdims=True)
        r_ref[...] = r_new
        return jnp.min(r_new)

    acc_ref[...] = jnp.zeros_like(acc_ref)
    r_ref[...] = jnp.zeros_like(r_ref)
    key = lax.broadcasted_iota(jnp.int32, (s_new, lanes), 0)
    qi = lax.broadcasted_iota(jnp.int32, (s_new, lanes), 1) % s_new
    rmin = block(kn_ref[0], vn_ref[0], tri_ref[0:s_new, 0:s_new], key < qi)

    def gather_heads(buf, slot):
        return jnp.concatenate(
            [buf[slot, pl.ds(h, tk, stride=n_heads), :] for h in range(n_heads)], axis=-1).astype(BF16)

    def cond(cr):
        c, rmin = cr
        return jnp.logical_and(c < nc, rmin < UNDERFLOW_BITS)

    def body(cr):
        c, _ = cr
        slot = c & 1
        wait_chunk(b, c, slot)

        @pl.when(c + 1 < nc)
        def _():
            start_chunk(b, c + 1, 1 - slot)

        rmin = block(gather_heads(kbuf, slot), gather_heads(vbuf, slot), tri_ref[...], None)
        return c + 1, rmin

    c_end, _ = lax.while_loop(cond, body, (jnp.int32(0), rmin))

    @pl.when(c_end < nc)
    def _():
        wait_chunk(b, c_end, c_end & 1)

    @pl.when(b + 1 < nb)
    def _():
        start_chunk(b + 1, 0, 0)

    acc = acc_ref[...]
    o_ref[b] = jnp.concatenate(
        [acc[h * s_new:(h + 1) * s_new, h * hd:(h + 1) * hd] for h in range(n_heads)], axis=-1)

    @pl.when(b == nb - 1)
    def _():
        rows = nb * s_new
        flat = lambda ref: ref[...].reshape(rows, d)
        y = _merge_out(flat(o_ref), flat(zs_ref), flat(gs_ref), flat(mc_ref), flat(x_ref), wso_ref, wo_ref)
        y_ref[...] = y.reshape(nb, s_new, d)


def _attn_sample(qbd, kn, vn, ck, cv, tri, zs, gs, mc, x, wso, wo, *, n_heads):
    b, s_new, d = kn.shape
    hd = d // n_heads
    tk = SAMPLE_KEY_TILE
    lanes = qbd.shape[2]
    per_b = lambda shape: pl.BlockSpec((1,) + shape, lambda i: (i, 0, 0))
    whole = pl.BlockSpec((b, s_new, d), lambda i: (0, 0, 0), pipeline_mode=pl.Buffered(1))
    hbm = pl.BlockSpec(memory_space=pl.ANY)

    def const(shape):
        return pl.BlockSpec(shape, lambda i: (0,) * len(shape), pipeline_mode=pl.Buffered(1))

    return pl.pallas_call(
        functools.partial(_attn_sample_kernel, n_heads=n_heads, tk=tk),
        grid=(b,),
        in_specs=[per_b((d, lanes)), per_b((s_new, d)), per_b((s_new, d)), hbm, hbm,
                  const(tri.shape), whole, whole, whole, whole, const(wso.shape), const(wo.shape)],
        out_specs=pl.BlockSpec((b, s_new, d), lambda i: (0, 0, 0)),
        out_shape=jax.ShapeDtypeStruct((b, s_new, d), F32),
        scratch_shapes=[pltpu.VMEM((2, tk * n_heads, hd), F32), pltpu.VMEM((2, tk * n_heads, hd), F32),
                        pltpu.SemaphoreType.DMA((2, 2)),
                        pltpu.VMEM((lanes, d), F32), pltpu.VMEM((1, lanes), F32),
                        pltpu.VMEM((b, s_new, d), F32)],
        compiler_params=pltpu.CompilerParams(
            dimension_semantics=("arbitrary",),
            vmem_limit_bytes=V7X_VMEM_LIMIT_BYTES),
        name="attn_sample",
    )(qbd, kn, vn, ck, cv, tri, zs, gs, mc, x, wso, wo)


def kernel(x_prompt, x_sample, cache_k, cache_v, state_conv, meta_tokens, norm_g, w_in, b_in,
           q_norm_g, k_norm_g, conv_w, w_conv_o, w_sb_o, w_o):
    depth, dec_b, p_len, n_heads, hd = cache_k.shape
    assert depth == 1, "single-layer stack only"
    b, t, d = x_prompt.shape
    s_new = x_sample.shape[1]
    n_meta = meta_tokens.shape[0]
    assert n_heads * hd == d and conv_w.shape[1] == 3 and w_in.shape[2] == N_SPLITS * d
    assert n_meta == s_new, "meta tokens and running streams share one projection call"
    assert t % ATTN_TILE == 0 and t % PROMPT_ROW_TILE == 0 and p_len % SAMPLE_KEY_TILE == 0

    weights = (norm_g[0][None], w_in[0].astype(BF16), b_in[0][None], q_norm_g[0][None],
               k_norm_g[0][None], conv_w[0], w_conv_o[0].astype(BF16))
    wso = w_sb_o[0].astype(BF16)
    wo = w_o[0].astype(BF16)

    x_ms = jnp.concatenate([meta_tokens[None], x_sample], axis=0)
    prev_ms = jnp.concatenate([jnp.zeros((1, 2, d), F32), state_conv[0]], axis=0)
    ms = _proj(x_ms, prev_ms, *weights, block_b=dec_b + 1, block_t=s_new, n_heads=n_heads)

    prev_p = jnp.broadcast_to(ms["cs"][0:1], (b, 2, d))
    pr = _proj(x_prompt, prev_p, *weights, block_b=1, block_t=PROMPT_ROW_TILE, n_heads=n_heads,
               lead_kv=(ms["k32"][0:1], ms["v32"][0:1]))

    idx = jnp.arange(ATTN_TILE)
    tri_q = (idx[:, None] > idx[None, :]).astype(BF16)
    meta_pad = 128
    km = jnp.zeros((meta_pad, d), BF16).at[:n_meta].set(ms["k"][0])
    vm = jnp.zeros((meta_pad, d), BF16).at[:n_meta].set(ms["v"][0])
    y_prompt = _attn_prompt(pr["q"], pr["k"], pr["v"], km, vm, tri_q, pr["zs"], pr["gs"], pr["mc"],
                            x_prompt, wso, wo, n_heads=n_heads, n_meta=n_meta)

    idk = jnp.arange(SAMPLE_KEY_TILE)
    tri_k = (idk[None, :] > idk[:, None]).astype(BF16)
    qs = ms["q"][1:].reshape(dec_b, s_new, n_heads, hd)
    eye = jnp.eye(n_heads, dtype=BF16)
    qbd = jnp.einsum("bihd,hg->bhdgi", qs, eye).reshape(dec_b, d, n_heads * s_new)
    ck = cache_k[0].reshape(dec_b, p_len * n_heads, hd)
    cv = cache_v[0].reshape(dec_b, p_len * n_heads, hd)
    y_sample = _attn_sample(qbd, ms["k"][1:], ms["v"][1:], ck, cv, tri_k, ms["zs"][1:], ms["gs"][1:],
                            ms["mc"][1:], x_sample, wso, wo, n_heads=n_heads)

    def heads(a):
        return a.reshape(1, a.shape[0], a.shape[1] // n_heads, n_heads, hd)

    return (y_prompt, y_sample, heads(pr["k32"]), heads(pr["v32"]), pr["cs"][None],
            heads(ms["k32"][1:]), heads(ms["v32"][1:]), ms["cs"][1:][None])
```

```python
import functools
import math

import jax
import jax.numpy as jnp
from jax import lax
from jax.experimental import pallas as pl
from jax.experimental.pallas import tpu as pltpu

EPS = 1e-6
N_SPLITS = 10
UNDERFLOW_BITS = 160.0
EXP2_CLAMP = 64.0
LOG2_E = math.log2(math.e)
V7X_VMEM_LIMIT_BYTES = 60000 * 1024
V7X_SUBLANES = 8
ATTN_TILE = 256
META_KEY_PAD = 128
SAMPLE_KEY_TILE = 256

F32 = jnp.float32
BF16 = jnp.bfloat16


def _softplus2(z2):
    return jnp.maximum(z2, jnp.log(1.0 + jnp.exp2(jnp.minimum(z2, EXP2_CLAMP))) * LOG2_E)


def _run(*generators):
    live = list(generators)
    while live:
        for g in list(live):
            if next(g, StopIteration) is StopIteration:
                live.remove(g)


def _project(x_ref, prev_ref, carry_ref, cs_ref, w, out, *, first_tile, n_heads):
    ng_ref, win_ref, bin_ref, qg_ref, kg_ref, cw_ref, wco_ref = w
    bt, tt, d = x_ref.shape
    rows = bt * tt
    hd = d // n_heads

    x = x_ref[...].reshape(rows, d)
    xn = x * lax.rsqrt(jnp.mean(x * x, axis=-1, keepdims=True) + EPS) * ng_ref[...]
    xn = xn.astype(BF16)

    def proj(i):
        wi = win_ref[:, i * d:(i + 1) * d]
        return jnp.dot(xn, wi, preferred_element_type=F32) + bin_ref[:, i * d:(i + 1) * d]

    def head_norm(p, g_ref, scale):
        outs = []
        for h in range(n_heads):
            ph = p[:, h * hd:(h + 1) * hd]
            n = ph * lax.rsqrt(jnp.mean(ph * ph, axis=-1, keepdims=True) + EPS) * g_ref[...]
            outs.append(n * scale if scale is not None else n)
        return jnp.concatenate(outs, axis=-1)

    out["q"] = head_norm(proj(4), qg_ref, LOG2_E * hd ** -0.5).astype(BF16)
    out["k"] = head_norm(proj(5), kg_ref, None)
    out["v"] = proj(6)
    yield

    u = proj(1) * proj(2)

    @pl.when(first_tile)
    def _():
        carry_ref[...] = prev_ref[...]

    prev = carry_ref[...]
    p0 = prev[:, 0:1, :]
    p1 = prev[:, 1:2, :]
    u3 = u.reshape(bt, tt, d)
    tidx = lax.broadcasted_iota(jnp.int32, (bt, tt, d), 1)
    r1 = pltpu.roll(u, 1, 0).reshape(bt, tt, d)
    r2 = pltpu.roll(u, 2, 0).reshape(bt, tt, d)
    um1 = jnp.where(tidx == 0, p1, r1)
    um2 = jnp.where(tidx == 0, p0, jnp.where(tidx == 1, p1, r2))
    cw = cw_ref[...]
    conv = cw[0:1, :] * um2 + cw[1:2, :] * um1 + cw[2:3, :] * u3
    tail = u3[:, tt - 2:tt, :]
    carry_ref[...] = tail
    cs_ref[...] = tail
    yield

    yc = (jax.nn.silu(proj(3)) * (proj(0) * conv.reshape(rows, d))).astype(BF16)
    yield

    out["mc"] = jax.nn.sigmoid(proj(8)) * jnp.dot(yc, wco_ref[...], preferred_element_type=F32)
    yield

    out["zs"] = jax.nn.silu(proj(7))
    out["gs"] = jax.nn.sigmoid(proj(9))


class _KvSlabs:
    def __init__(self, kst_ref, vst_ref, k32_hbm, v32_hbm, sems, *, step, n_steps, n_heads):
        self.pairs = ((kst_ref, k32_hbm), (vst_ref, v32_hbm))
        self.sems, self.step, self.n_steps, self.n_heads = sems, step, n_steps, n_heads
        self.slot = step & 1

    def _copies(self, slot, b0=0, r0=0):
        _, bt, srows, _ = self.pairs[0][0].shape
        return [pltpu.make_async_copy(
            st.at[slot], out.at[pl.ds(b0, bt), pl.ds(r0, srows), :], self.sems.at[i, slot])
            for i, (st, out) in enumerate(self.pairs)]

    def wait_reusable(self):
        @pl.when(self.step >= 2)
        def _():
            for copy in self._copies(self.slot):
                copy.wait()

    def stage(self, which, val):
        st = self.pairs[which][0]
        _, bt, srows, hd = st.shape
        tt = srows // self.n_heads
        for h in range(self.n_heads):
            st[self.slot, :, pl.ds(h, tt, stride=self.n_heads), :] = (
                val[:, h * hd:(h + 1) * hd].reshape(bt, tt, hd))

    def start(self, b0, pos0):
        r0 = pl.multiple_of(pos0 * self.n_heads, V7X_SUBLANES)
        for copy in self._copies(self.slot, b0, r0):
            copy.start()

    def drain_on_last_step(self):
        @pl.when(self.step == self.n_steps - 1)
        def _():
            for copy in self._copies(self.slot):
                copy.wait()

        @pl.when(jnp.logical_and(self.step == self.n_steps - 1, self.n_steps >= 2))
        def _():
            for copy in self._copies(1 - self.slot):
                copy.wait()


def _merge_out(o, zs, gs, mc, x, wso_ref, wo_ref):
    ys = (zs * o).astype(BF16)
    m = mc + gs * jnp.dot(ys, wso_ref[...], preferred_element_type=F32)
    return x + jnp.dot(m.astype(BF16), wo_ref[...], preferred_element_type=F32)


def _gather_heads(buf, slot, n_rows, n_heads):
    return jnp.concatenate(
        [buf[slot, pl.ds(h, n_rows, stride=n_heads), :] for h in range(n_heads)], axis=-1).astype(BF16)


def _proj_kernel(x_ref, prev_ref, ng_ref, win_ref, bin_ref, qg_ref, kg_ref, cw_ref, wco_ref,
                 q_ref, k_ref, v_ref, k32_hbm, v32_hbm, mc_ref, zs_ref, gs_ref, cs_ref,
                 carry_ref, kst_ref, vst_ref, sems, *, n_heads):
    bi = pl.program_id(0)
    t = pl.program_id(1)
    nt = pl.num_programs(1)
    bt, tt, d = x_ref.shape
    slabs = _KvSlabs(kst_ref, vst_ref, k32_hbm, v32_hbm, sems, step=bi * nt + t,
                     n_steps=pl.num_programs(0) * nt, n_heads=n_heads)
    slabs.wait_reusable()
    p = {}
    _run(_project(x_ref, prev_ref, carry_ref, cs_ref,
                  (ng_ref, win_ref, bin_ref, qg_ref, kg_ref, cw_ref, wco_ref), p,
                  first_tile=t == 0, n_heads=n_heads))
    slabs.stage(0, p["k"])
    slabs.stage(1, p["v"])
    q_ref[...] = p["q"].reshape(bt, tt, d)
    k_ref[...] = p["k"].astype(BF16).reshape(bt, tt, d)
    v_ref[...] = p["v"].astype(BF16).reshape(bt, tt, d)
    mc_ref[...] = p["mc"].reshape(bt, tt, d)
    zs_ref[...] = p["zs"].reshape(bt, tt, d)
    gs_ref[...] = p["gs"].reshape(bt, tt, d)
    slabs.start(bi * bt, t * tt)
    slabs.drain_on_last_step()


def _proj(x, prev, ng, win, bin_, qg, kg, cw, wco, *, block_b, block_t, n_heads):
    b, t, d = x.shape
    hd = d // n_heads
    grid = (b // block_b, t // block_t)
    row_spec = pl.BlockSpec((block_b, block_t, d), lambda i, j: (i, j, 0))
    state_spec = pl.BlockSpec((block_b, 2, d), lambda i, j: (i, 0, 0))
    hbm = pl.BlockSpec(memory_space=pl.ANY)

    def const(shape):
        return pl.BlockSpec(shape, lambda i, j: (0,) * len(shape), pipeline_mode=pl.Buffered(1))

    bf = jax.ShapeDtypeStruct((b, t, d), BF16)
    f32 = jax.ShapeDtypeStruct((b, t, d), F32)
    kv32 = jax.ShapeDtypeStruct((b, t * n_heads, hd), F32)
    stage = pltpu.VMEM((2, block_b, block_t * n_heads, hd), F32)
    outs = pl.pallas_call(
        functools.partial(_proj_kernel, n_heads=n_heads),
        grid=grid,
        in_specs=[row_spec, state_spec, const((1, d)), const(win.shape), const(bin_.shape),
                  const(qg.shape), const(kg.shape), const(cw.shape), const(wco.shape)],
        out_specs=[row_spec] * 3 + [hbm, hbm] + [row_spec] * 3 + [state_spec],
        out_shape=[bf, bf, bf, kv32, kv32, f32, f32, f32, jax.ShapeDtypeStruct((b, 2, d), F32)],
        scratch_shapes=[pltpu.VMEM((block_b, 2, d), F32), stage, stage, pltpu.SemaphoreType.DMA((2, 2))],
        compiler_params=pltpu.CompilerParams(
            dimension_semantics=("arbitrary", "arbitrary"),
            vmem_limit_bytes=V7X_VMEM_LIMIT_BYTES),
        name="proj",
    )(x, prev, ng, win, bin_, qg, kg, cw, wco)
    names = ("q", "k", "v", "k32", "v32", "mc", "zs", "gs", "cs")
    return dict(zip(names, outs))


def _prompt_layer_kernel(leadk_ref, leadv_ref, km_ref, vm_ref, tri_ref, x_ref, prev_ref,
                         ng_ref, win_ref, bin_ref, qg_ref, kg_ref, cw_ref, wco_ref, wso_ref, wo_ref,
                         y_ref, k32_hbm, v32_hbm, cs_ref,
                         carry_ref, kst_ref, vst_ref, slab_sems, kring, vring, acc_ref, r_ref,
                         kbuf, vbuf, old_sems, *, n_heads, n_meta):
    bi = pl.program_id(0)
    t = pl.program_id(1)
    nt = pl.num_programs(1)
    step = bi * nt + t
    _, tq, d = x_ref.shape
    hd = d // n_heads
    cur = t & 1

    slabs = _KvSlabs(kst_ref, vst_ref, k32_hbm, v32_hbm, slab_sems, step=step,
                     n_steps=pl.num_programs(0) * nt, n_heads=n_heads)
    slabs.wait_reusable()

    @pl.when(t == 0)
    def _():
        for lead_ref, out_hbm in ((leadk_ref, k32_hbm), (leadv_ref, v32_hbm)):
            copy = pltpu.make_async_copy(
                lead_ref, out_hbm.at[pl.ds(bi, 1), pl.ds(0, n_meta * n_heads), :], old_sems.at[0, 0])
            copy.start()
            copy.wait()

    @pl.when(step == 0)
    def _():
        kring[...] = jnp.zeros_like(kring)
        vring[...] = jnp.zeros_like(vring)

    p = {}
    projection = _project(x_ref, prev_ref, carry_ref, cs_ref,
                          (ng_ref, win_ref, bin_ref, qg_ref, kg_ref, cw_ref, wco_ref), p,
                          first_tile=t == 0, n_heads=n_heads)
    next(projection)
    slabs.stage(0, p["k"])
    slabs.stage(1, p["v"])
    slabs.start(bi, n_meta + t * tq)
    kring[cur] = p["k"].astype(BF16)
    vring[cur] = p["v"].astype(BF16)
    q = p["q"]

    acc_ref[...] = jnp.zeros_like(acc_ref)
    r_ref[...] = jnp.zeros_like(r_ref)
    low = {}

    def block(kblk, vblk, tri, mask):
        logb, spms = [], []
        for h in range(n_heads):
            hs = slice(h * hd, (h + 1) * hd)
            z2 = lax.dot_general(q[:, hs], kblk[:, hs], (((1,), (1,)), ((), ())),
                                 preferred_element_type=F32)
            sp = _softplus2(z2)
            logb.append(z2 - sp)
            spms.append(sp if mask is None else jnp.where(mask, sp, 0.0))
        yield
        suf_all = jnp.dot(jnp.concatenate([s.astype(BF16) for s in spms], axis=0), tri,
                          preferred_element_type=F32)
        yield
        r_low = None
        for h in range(n_heads):
            hs = slice(h * hd, (h + 1) * hd)
            r = r_ref[h]
            a = jnp.exp2(logb[h] - (suf_all[h * tq:(h + 1) * tq] + r))
            if mask is not None:
                a = jnp.where(mask, a, 0.0)
            acc_ref[h] += jnp.dot(a.astype(BF16), vblk[:, hs], preferred_element_type=F32)
            r_new = r + jnp.sum(spms[h], axis=-1, keepdims=True)
            r_ref[h] = r_new
            r_low = r_new if r_low is None else jnp.minimum(r_low, r_new)
        low["min"] = jnp.min(r_low)

    def newest_blocks():
        row = lax.broadcasted_iota(jnp.int32, (tq, tq), 0)
        col = lax.broadcasted_iota(jnp.int32, (tq, tq), 1)
        yield from block(kring[cur], vring[cur], tri_ref[...], col < row)
        yield
        yield from block(kring[1 - cur], vring[1 - cur], tri_ref[...], jnp.broadcast_to(t >= 1, (tq, tq)))

    _run(newest_blocks(), projection)
    rmin = low["min"]

    def old_copies(j, slot):
        rows = pl.ds(pl.multiple_of((n_meta + j * tq) * n_heads, V7X_SUBLANES), tq * n_heads)
        return (pltpu.make_async_copy(k32_hbm.at[bi, rows, :], kbuf.at[slot], old_sems.at[0, slot]),
                pltpu.make_async_copy(v32_hbm.at[bi, rows, :], vbuf.at[slot], old_sems.at[1, slot]))

    n_old = jnp.maximum(t - 1, 0)

    @pl.when(jnp.logical_and(n_old > 0, rmin < UNDERFLOW_BITS))
    def _():
        for copy in old_copies(t - 2, 0):
            copy.start()

    def cond(cr):
        c, rmin = cr
        return jnp.logical_and(c < n_old, rmin < UNDERFLOW_BITS)

    def body(cr):
        c, _ = cr
        slot = c & 1
        for copy in old_copies(t - 2 - c, slot):
            copy.wait()

        @pl.when(c + 1 < n_old)
        def _():
            for copy in old_copies(t - 3 - c, 1 - slot):
                copy.start()

        _run(block(_gather_heads(kbuf, slot, tq, n_heads), _gather_heads(vbuf, slot, tq, n_heads),
                   tri_ref[...], None))
        return c + 1, low["min"]

    c_end, rmin = lax.while_loop(cond, body, (jnp.int32(0), rmin))

    @pl.when(jnp.logical_and(c_end > 0, c_end < n_old))
    def _():
        for copy in old_copies(t - 2 - c_end, c_end & 1):
            copy.wait()

    @pl.when(rmin < UNDERFLOW_BITS)
    def _():
        wm = km_ref.shape[0]
        mcol = lax.broadcasted_iota(jnp.int32, (tq, wm), 1)
        _run(block(km_ref[...], vm_ref[...], tri_ref[0:wm, 0:wm], mcol < n_meta))

    o = jnp.concatenate([acc_ref[h] for h in range(n_heads)], axis=-1)
    y_ref[0] = _merge_out(o, p["zs"], p["gs"], p["mc"], x_ref[0], wso_ref, wo_ref)
    slabs.drain_on_last_step()


def _prompt_layer(x, prev, lead_k, lead_v, km, vm, tri, ng, win, bin_, qg, kg, cw, wco, wso, wo,
                  *, n_heads, n_meta):
    b, t, d = x.shape
    hd = d // n_heads
    tq = ATTN_TILE
    tile = pl.BlockSpec((1, tq, d), lambda i, j: (i, j, 0))
    state_spec = pl.BlockSpec((1, 2, d), lambda i, j: (i, 0, 0))
    hbm = pl.BlockSpec(memory_space=pl.ANY)

    def const(a):
        return pl.BlockSpec(a.shape, lambda i, j: (0,) * a.ndim, pipeline_mode=pl.Buffered(1))

    kv32 = jax.ShapeDtypeStruct((b, (n_meta + t) * n_heads, hd), F32)
    stage = pltpu.VMEM((2, 1, tq * n_heads, hd), F32)
    old = pltpu.VMEM((2, tq * n_heads, hd), F32)
    ring = pltpu.VMEM((2, tq, d), BF16)
    consts = (lead_k, lead_v, km, vm, tri)
    weights = (ng, win, bin_, qg, kg, cw, wco, wso, wo)
    return pl.pallas_call(
        functools.partial(_prompt_layer_kernel, n_heads=n_heads, n_meta=n_meta),
        grid=(b, t // tq),
        in_specs=[const(a) for a in consts] + [tile, state_spec] + [const(a) for a in weights],
        out_specs=[tile, hbm, hbm, state_spec],
        out_shape=[jax.ShapeDtypeStruct((b, t, d), F32), kv32, kv32, jax.ShapeDtypeStruct((b, 2, d), F32)],
        scratch_shapes=[pltpu.VMEM((1, 2, d), F32), stage, stage, pltpu.SemaphoreType.DMA((2, 2)),
                        ring, ring, pltpu.VMEM((n_heads, tq, hd), F32), pltpu.VMEM((n_heads, tq, 1), F32),
                        old, old, pltpu.SemaphoreType.DMA((2, 2))],
        compiler_params=pltpu.CompilerParams(
            dimension_semantics=("arbitrary", "arbitrary"),
            vmem_limit_bytes=V7X_VMEM_LIMIT_BYTES),
        name="prompt_layer",
    )(*consts, x, prev, *weights)


def _attn_sample_kernel(qbd_ref, kn_ref, vn_ref, ck_hbm, cv_hbm, tri_ref, zs_ref, gs_ref, mc_ref, x_ref,
                        wso_ref, wo_ref, y_ref, kbuf, vbuf, sems, acc_ref, r_ref, o_ref,
                        *, n_heads, tk):
    b = pl.program_id(0)
    nb = pl.num_programs(0)
    s_new = kn_ref.shape[1]
    d = kn_ref.shape[2]
    hd = d // n_heads
    lanes = qbd_ref.shape[2]
    p_len = ck_hbm.shape[1] // n_heads
    nc = p_len // tk

    def chunk_copies(bb, c, slot):
        r0 = pl.multiple_of((p_len - (c + 1) * tk) * n_heads, V7X_SUBLANES)
        rows = pl.ds(r0, tk * n_heads)
        return (pltpu.make_async_copy(ck_hbm.at[bb, rows, :], kbuf.at[slot], sems.at[0, slot]),
                pltpu.make_async_copy(cv_hbm.at[bb, rows, :], vbuf.at[slot], sems.at[1, slot]))

    def start_chunk(bb, c, slot):
        for cp in chunk_copies(bb, c, slot):
            cp.start()

    def wait_chunk(bb, c, slot):
        for cp in chunk_copies(bb, c, slot):
            cp.wait()

    @pl.when(b == 0)
    def _():
        start_chunk(b, 0, 0)

    def block(kblk, vblk, tri, mask):
        z2 = jnp.dot(kblk, qbd_ref[0], preferred_element_type=F32)
        sp = _softplus2(z2)
        spm = sp if mask is None else jnp.where(mask, sp, 0.0)
        suf = jnp.dot(tri, spm.astype(BF16), preferred_element_type=F32)
        r = r_ref[...]
        a = jnp.exp2((z2 - sp) - (suf + r))
        if mask is not None:
            a = jnp.where(mask, a, 0.0)
        acc_ref[...] += lax.dot_general(a.astype(BF16), vblk, (((0,), (0,)), ((), ())),
                                        preferred_element_type=F32)
        r_new = r + jnp.sum(spm, axis=0, keepdims=True)
        r_ref[...] = r_new
        return jnp.min(r_new)

    acc_ref[...] = jnp.zeros_like(acc_ref)
    r_ref[...] = jnp.zeros_like(r_ref)
    key = lax.broadcasted_iota(jnp.int32, (s_new, lanes), 0)
    qi = lax.broadcasted_iota(jnp.int32, (s_new, lanes), 1) % s_new
    rmin = block(kn_ref[0], vn_ref[0], tri_ref[0:s_new, 0:s_new], key < qi)

    def cond(cr):
        c, rmin = cr
        return jnp.logical_and(c < nc, rmin < UNDERFLOW_BITS)

    def body(cr):
        c, _ = cr
        slot = c & 1
        wait_chunk(b, c, slot)

        @pl.when(c + 1 < nc)
        def _():
            start_chunk(b, c + 1, 1 - slot)

        rmin = block(_gather_heads(kbuf, slot, tk, n_heads), _gather_heads(vbuf, slot, tk, n_heads),
                     tri_ref[...], None)
        return c + 1, rmin

    c_end, _ = lax.while_loop(cond, body, (jnp.int32(0), rmin))

    @pl.when(c_end < nc)
    def _():
        wait_chunk(b, c_end, c_end & 1)

    @pl.when(b + 1 < nb)
    def _():
        start_chunk(b + 1, 0, 0)

    acc = acc_ref[...]
    o_ref[b] = jnp.concatenate(
        [acc[h * s_new:(h + 1) * s_new, h * hd:(h + 1) * hd] for h in range(n_heads)], axis=-1)

    @pl.when(b == nb - 1)
    def _():
        rows = nb * s_new
        lead = zs_ref.shape[0] - nb
        flat = lambda ref, skip: ref[skip:].reshape(rows, d)
        y = _merge_out(flat(o_ref, 0), flat(zs_ref, lead), flat(gs_ref, lead), flat(mc_ref, lead),
                       flat(x_ref, 0), wso_ref, wo_ref)
        y_ref[...] = y.reshape(nb, s_new, d)


def _attn_sample(qbd, kn, vn, ck, cv, tri, zs, gs, mc, x, wso, wo, *, n_heads):
    b, s_new, d = x.shape
    lead = kn.shape[0] - b
    hd = d // n_heads
    tk = SAMPLE_KEY_TILE
    lanes = qbd.shape[2]
    per_b = lambda shape, skip=0: pl.BlockSpec((1,) + shape, lambda i: (i + skip, 0, 0))
    whole = lambda a: pl.BlockSpec(a.shape, lambda i: (0, 0, 0), pipeline_mode=pl.Buffered(1))
    hbm = pl.BlockSpec(memory_space=pl.ANY)

    def const(shape):
        return pl.BlockSpec(shape, lambda i: (0,) * len(shape), pipeline_mode=pl.Buffered(1))

    return pl.pallas_call(
        functools.partial(_attn_sample_kernel, n_heads=n_heads, tk=tk),
        grid=(b,),
        in_specs=[per_b((d, lanes)), per_b((s_new, d), lead), per_b((s_new, d), lead), hbm, hbm,
                  const(tri.shape), whole(zs), whole(gs), whole(mc), whole(x),
                  const(wso.shape), const(wo.shape)],
        out_specs=pl.BlockSpec((b, s_new, d), lambda i: (0, 0, 0)),
        out_shape=jax.ShapeDtypeStruct((b, s_new, d), F32),
        scratch_shapes=[pltpu.VMEM((2, tk * n_heads, hd), F32), pltpu.VMEM((2, tk * n_heads, hd), F32),
                        pltpu.SemaphoreType.DMA((2, 2)),
                        pltpu.VMEM((lanes, d), F32), pltpu.VMEM((1, lanes), F32),
                        pltpu.VMEM((b, s_new, d), F32)],
        compiler_params=pltpu.CompilerParams(
            dimension_semantics=("arbitrary",),
            vmem_limit_bytes=V7X_VMEM_LIMIT_BYTES),
        name="attn_sample",
    )(qbd, kn, vn, ck, cv, tri, zs, gs, mc, x, wso, wo)


def kernel(x_prompt, x_sample, cache_k, cache_v, state_conv, meta_tokens, norm_g, w_in, b_in,
           q_norm_g, k_norm_g, conv_w, w_conv_o, w_sb_o, w_o):
    depth, dec_b, p_len, n_heads, hd = cache_k.shape
    assert depth == 1, "single-layer stack only"
    b, t, d = x_prompt.shape
    s_new = x_sample.shape[1]
    n_meta = meta_tokens.shape[0]
    assert n_heads * hd == d and conv_w.shape[1] == 3 and w_in.shape[2] == N_SPLITS * d
    assert n_meta == s_new, "meta tokens and running streams share one projection call"
    assert t % ATTN_TILE == 0 and p_len % SAMPLE_KEY_TILE == 0 and n_meta <= META_KEY_PAD

    weights = (norm_g[0][None], w_in[0].astype(BF16), b_in[0][None], q_norm_g[0][None],
               k_norm_g[0][None], conv_w[0], w_conv_o[0].astype(BF16))
    wso = w_sb_o[0].astype(BF16)
    wo = w_o[0].astype(BF16)

    x_ms = jnp.concatenate([meta_tokens[None], x_sample], axis=0)
    prev_ms = jnp.concatenate([jnp.zeros((1, 2, d), F32), state_conv[0]], axis=0)
    ms = _proj(x_ms, prev_ms, *weights, block_b=dec_b + 1, block_t=s_new, n_heads=n_heads)

    idx = jnp.arange(ATTN_TILE)
    tri_q = (idx[:, None] > idx[None, :]).astype(BF16)
    km = jnp.zeros((META_KEY_PAD, d), BF16).at[:n_meta].set(ms["k"][0])
    vm = jnp.zeros((META_KEY_PAD, d), BF16).at[:n_meta].set(ms["v"][0])
    prev_p = jnp.broadcast_to(ms["cs"][0:1], (b, 2, d))
    y_prompt, k_prompt, v_prompt, cs_prompt = _prompt_layer(
        x_prompt, prev_p, ms["k32"][0:1], ms["v32"][0:1], km, vm, tri_q, *weights, wso, wo,
        n_heads=n_heads, n_meta=n_meta)

    idk = jnp.arange(SAMPLE_KEY_TILE)
    tri_k = (idk[None, :] > idk[:, None]).astype(BF16)
    qs = ms["q"][1:].reshape(dec_b, s_new, n_heads, hd)
    eye = jnp.eye(n_heads, dtype=BF16)
    qbd = jnp.einsum("bihd,hg->bhdgi", qs, eye).reshape(dec_b, d, n_heads * s_new)
    ck = cache_k[0].reshape(dec_b, p_len * n_heads, hd)
    cv = cache_v[0].reshape(dec_b, p_len * n_heads, hd)
    y_sample = _attn_sample(qbd, ms["k"], ms["v"], ck, cv, tri_k, ms["zs"], ms["gs"], ms["mc"],
                            x_sample, wso, wo, n_heads=n_heads)

    def heads(a):
        return a.reshape(1, a.shape[0], a.shape[1] // n_heads, n_heads, hd)

    return (y_prompt, y_sample, heads(k_prompt), heads(v_prompt), cs_prompt[None],
            heads(ms["k32"][1:]), heads(ms["v32"][1:]), ms["cs"][1:][None])
```

```python
import functools
import math

import jax
import jax.numpy as jnp
from jax import lax
from jax.experimental import pallas as pl
from jax.experimental.pallas import tpu as pltpu

EPS = 1e-6
N_SPLITS = 10
UNDERFLOW_BITS = 160.0
EXP2_CLAMP = 64.0
LOG2_E = math.log2(math.e)
V7X_VMEM_LIMIT_BYTES = 60000 * 1024
V7X_SUBLANES = 8
PROMPT_ROW_TILE = 256
ATTN_TILE = 256
META_KEY_PAD = 128
SAMPLE_KEY_TILE = 256

F32 = jnp.float32
BF16 = jnp.bfloat16


def _softplus2(z2):
    return jnp.maximum(z2, jnp.log(1.0 + jnp.exp2(jnp.minimum(z2, EXP2_CLAMP))) * LOG2_E)


class _KvSlabs:
    def __init__(self, kst_ref, vst_ref, k32_hbm, v32_hbm, sems, *, step, n_steps, n_heads):
        self.pairs = ((kst_ref, k32_hbm), (vst_ref, v32_hbm))
        self.sems, self.step, self.n_steps, self.n_heads = sems, step, n_steps, n_heads
        self.slot = step & 1

    def _copies(self, slot, b0=0, r0=0):
        _, bt, srows, _ = self.pairs[0][0].shape
        return [pltpu.make_async_copy(
            st.at[slot], out.at[pl.ds(b0, bt), pl.ds(r0, srows), :], self.sems.at[i, slot])
            for i, (st, out) in enumerate(self.pairs)]

    def wait_reusable(self):
        @pl.when(self.step >= 2)
        def _():
            for copy in self._copies(self.slot):
                copy.wait()

    def stage(self, which, val):
        st = self.pairs[which][0]
        _, bt, srows, hd = st.shape
        tt = srows // self.n_heads
        for h in range(self.n_heads):
            st[self.slot, :, pl.ds(h, tt, stride=self.n_heads), :] = (
                val[:, h * hd:(h + 1) * hd].reshape(bt, tt, hd))

    def start(self, b0, pos0):
        r0 = pl.multiple_of(pos0 * self.n_heads, V7X_SUBLANES)
        for copy in self._copies(self.slot, b0, r0):
            copy.start()

    def drain_on_last_step(self):
        @pl.when(self.step == self.n_steps - 1)
        def _():
            for copy in self._copies(self.slot):
                copy.wait()

        @pl.when(jnp.logical_and(self.step == self.n_steps - 1, self.n_steps >= 2))
        def _():
            for copy in self._copies(1 - self.slot):
                copy.wait()


def _proj_kernel(*refs, n_heads, n_lead):
    lead_refs, refs = (refs[:2], refs[2:]) if n_lead else ((), refs)
    (x_ref, prev_ref, ng_ref, win_ref, bin_ref, qg_ref, kg_ref, cw_ref, wco_ref,
     q_ref, k_ref, v_ref, k32_hbm, v32_hbm, mc_ref, zs_ref, gs_ref, cs_ref,
     carry_ref, kst_ref, vst_ref, sems, lead_sem) = refs
    bi = pl.program_id(0)
    t = pl.program_id(1)
    nt = pl.num_programs(1)
    bt, tt, d = x_ref.shape
    rows = bt * tt
    hd = d // n_heads
    slabs = _KvSlabs(kst_ref, vst_ref, k32_hbm, v32_hbm, sems, step=bi * nt + t,
                     n_steps=pl.num_programs(0) * nt, n_heads=n_heads)
    slabs.wait_reusable()

    if n_lead:
        @pl.when(t == 0)
        def _():
            for lead_ref, out_hbm in zip(lead_refs, (k32_hbm, v32_hbm)):
                dst = out_hbm.at[pl.ds(bi * bt, bt), pl.ds(0, n_lead * n_heads), :]
                copy = pltpu.make_async_copy(lead_ref, dst, lead_sem.at[0])
                copy.start()
                copy.wait()

    x = x_ref[...].reshape(rows, d)
    xn = x * lax.rsqrt(jnp.mean(x * x, axis=-1, keepdims=True) + EPS) * ng_ref[...]
    xn = xn.astype(BF16)

    def proj(i):
        w = win_ref[:, i * d:(i + 1) * d]
        return jnp.dot(xn, w, preferred_element_type=F32) + bin_ref[:, i * d:(i + 1) * d]

    u = proj(1) * proj(2)

    @pl.when(t == 0)
    def _():
        carry_ref[...] = prev_ref[...]

    prev = carry_ref[...]
    p0 = prev[:, 0:1, :]
    p1 = prev[:, 1:2, :]
    u3 = u.reshape(bt, tt, d)
    tidx = lax.broadcasted_iota(jnp.int32, (bt, tt, d), 1)
    r1 = pltpu.roll(u, 1, 0).reshape(bt, tt, d)
    r2 = pltpu.roll(u, 2, 0).reshape(bt, tt, d)
    um1 = jnp.where(tidx == 0, p1, r1)
    um2 = jnp.where(tidx == 0, p0, jnp.where(tidx == 1, p1, r2))
    cw = cw_ref[...]
    conv = cw[0:1, :] * um2 + cw[1:2, :] * um1 + cw[2:3, :] * u3
    tail = u3[:, tt - 2:tt, :]
    carry_ref[...] = tail
    cs_ref[...] = tail

    yc = (jax.nn.silu(proj(3)) * (proj(0) * conv.reshape(rows, d))).astype(BF16)
    gc = jax.nn.sigmoid(proj(8))

    def head_norm(p, g_ref, scale):
        outs = []
        for h in range(n_heads):
            ph = p[:, h * hd:(h + 1) * hd]
            n = ph * lax.rsqrt(jnp.mean(ph * ph, axis=-1, keepdims=True) + EPS) * g_ref[...]
            outs.append(n * scale if scale is not None else n)
        return jnp.concatenate(outs, axis=-1)

    qn = head_norm(proj(4), qg_ref, LOG2_E * hd ** -0.5)
    q_ref[...] = qn.astype(BF16).reshape(bt, tt, d)
    kn = head_norm(proj(5), kg_ref, None)
    slabs.stage(0, kn)
    k_ref[...] = kn.astype(BF16).reshape(bt, tt, d)
    vv = proj(6)
    slabs.stage(1, vv)
    v_ref[...] = vv.astype(BF16).reshape(bt, tt, d)
    zs_ref[...] = jax.nn.silu(proj(7)).reshape(bt, tt, d)
    gs_ref[...] = jax.nn.sigmoid(proj(9)).reshape(bt, tt, d)
    mc_ref[...] = (gc * jnp.dot(yc, wco_ref[...], preferred_element_type=F32)).reshape(bt, tt, d)

    slabs.start(bi * bt, n_lead + t * tt)
    slabs.drain_on_last_step()


def _proj(x, prev, ng, win, bin_, qg, kg, cw, wco, *, block_b, block_t, n_heads, lead_kv=()):
    b, t, d = x.shape
    hd = d // n_heads
    n_lead = lead_kv[0].shape[1] // n_heads if lead_kv else 0
    assert not lead_kv or block_b == 1
    grid = (b // block_b, t // block_t)
    row_spec = pl.BlockSpec((block_b, block_t, d), lambda i, j: (i, j, 0))
    state_spec = pl.BlockSpec((block_b, 2, d), lambda i, j: (i, 0, 0))
    hbm = pl.BlockSpec(memory_space=pl.ANY)

    def const(shape):
        return pl.BlockSpec(shape, lambda i, j: (0,) * len(shape), pipeline_mode=pl.Buffered(1))

    bf = jax.ShapeDtypeStruct((b, t, d), BF16)
    f32 = jax.ShapeDtypeStruct((b, t, d), F32)
    kv32 = jax.ShapeDtypeStruct((b, (n_lead + t) * n_heads, hd), F32)
    stage = pltpu.VMEM((2, block_b, block_t * n_heads, hd), F32)
    outs = pl.pallas_call(
        functools.partial(_proj_kernel, n_heads=n_heads, n_lead=n_lead),
        grid=grid,
        in_specs=[const(a.shape) for a in lead_kv]
        + [row_spec, state_spec, const((1, d)), const(win.shape), const(bin_.shape),
           const(qg.shape), const(kg.shape), const(cw.shape), const(wco.shape)],
        out_specs=[row_spec] * 3 + [hbm, hbm] + [row_spec] * 3 + [state_spec],
        out_shape=[bf, bf, bf, kv32, kv32, f32, f32, f32, jax.ShapeDtypeStruct((b, 2, d), F32)],
        scratch_shapes=[pltpu.VMEM((block_b, 2, d), F32), stage, stage,
                        pltpu.SemaphoreType.DMA((2, 2)), pltpu.SemaphoreType.DMA((1,))],
        compiler_params=pltpu.CompilerParams(
            dimension_semantics=("arbitrary", "arbitrary"),
            vmem_limit_bytes=V7X_VMEM_LIMIT_BYTES),
        name="proj",
    )(*lead_kv, x, prev, ng, win, bin_, qg, kg, cw, wco)
    names = ("q", "k", "v", "k32", "v32", "mc", "zs", "gs", "cs")
    return dict(zip(names, outs))


def _merge_out(o, zs, gs, mc, x, wso_ref, wo_ref):
    ys = (zs * o).astype(BF16)
    m = mc + gs * jnp.dot(ys, wso_ref[...], preferred_element_type=F32)
    return x + jnp.dot(m.astype(BF16), wo_ref[...], preferred_element_type=F32)


def _attn_prompt_kernel(q_ref, kcur_ref, vcur_ref, kprev_ref, vprev_ref, k_hbm, v_hbm, km_ref, vm_ref,
                        tri_ref, zs_ref, gs_ref, mc_ref, x_ref, wso_ref, wo_ref, y_ref,
                        acc_ref, r_ref, kbuf, vbuf, sems, *, n_heads, n_meta):
    bi = pl.program_id(0)
    t = pl.program_id(1)
    tq = q_ref.shape[1]
    d = q_ref.shape[2]
    hd = d // n_heads

    acc_ref[...] = jnp.zeros_like(acc_ref)
    r_ref[...] = jnp.zeros_like(r_ref)

    def block(kblk, vblk, tri, mask):
        logb, spms = [], []
        for h in range(n_heads):
            hs = slice(h * hd, (h + 1) * hd)
            z2 = lax.dot_general(q_ref[0, :, hs], kblk[:, hs], (((1,), (1,)), ((), ())),
                                 preferred_element_type=F32)
            sp = _softplus2(z2)
            logb.append(z2 - sp)
            spms.append(sp if mask is None else jnp.where(mask, sp, 0.0))
        suf_all = jnp.dot(jnp.concatenate([s.astype(BF16) for s in spms], axis=0), tri,
                          preferred_element_type=F32)
        r_low = None
        for h in range(n_heads):
            hs = slice(h * hd, (h + 1) * hd)
            r = r_ref[h]
            a = jnp.exp2(logb[h] - (suf_all[h * tq:(h + 1) * tq] + r))
            if mask is not None:
                a = jnp.where(mask, a, 0.0)
            acc_ref[h] += jnp.dot(a.astype(BF16), vblk[:, hs], preferred_element_type=F32)
            r_new = r + jnp.sum(spms[h], axis=-1, keepdims=True)
            r_ref[h] = r_new
            r_low = r_new if r_low is None else jnp.minimum(r_low, r_new)
        return jnp.min(r_low)

    row = lax.broadcasted_iota(jnp.int32, (tq, tq), 0)
    col = lax.broadcasted_iota(jnp.int32, (tq, tq), 1)
    block(kcur_ref[0], vcur_ref[0], tri_ref[...], col < row)
    rmin = block(kprev_ref[0], vprev_ref[0], tri_ref[...], jnp.broadcast_to(t >= 1, (tq, tq)))

    def old_copies(c, slot):
        rows = pl.ds(pl.multiple_of((t - 2 - c) * tq, tq), tq)
        return (pltpu.make_async_copy(k_hbm.at[bi, rows, :], kbuf.at[slot], sems.at[0, slot]),
                pltpu.make_async_copy(v_hbm.at[bi, rows, :], vbuf.at[slot], sems.at[1, slot]))

    n_old = jnp.maximum(t - 1, 0)

    @pl.when(jnp.logical_and(n_old > 0, rmin < UNDERFLOW_BITS))
    def _():
        for copy in old_copies(0, 0):
            copy.start()

    def cond(cr):
        c, rmin = cr
        return jnp.logical_and(c < n_old, rmin < UNDERFLOW_BITS)

    def body(cr):
        c, _ = cr
        slot = c & 1
        for copy in old_copies(c, slot):
            copy.wait()

        @pl.when(c + 1 < n_old)
        def _():
            for copy in old_copies(c + 1, 1 - slot):
                copy.start()

        return c + 1, block(kbuf[slot], vbuf[slot], tri_ref[...], None)

    c_end, rmin = lax.while_loop(cond, body, (jnp.int32(0), rmin))

    @pl.when(jnp.logical_and(c_end > 0, c_end < n_old))
    def _():
        for copy in old_copies(c_end, c_end & 1):
            copy.wait()

    @pl.when(rmin < UNDERFLOW_BITS)
    def _():
        wm = km_ref.shape[0]
        mcol = lax.broadcasted_iota(jnp.int32, (tq, wm), 1)
        block(km_ref[...], vm_ref[...], tri_ref[0:wm, 0:wm], mcol < n_meta)

    o = jnp.concatenate([acc_ref[h] for h in range(n_heads)], axis=-1)
    y_ref[0] = _merge_out(o, zs_ref[0], gs_ref[0], mc_ref[0], x_ref[0], wso_ref, wo_ref)


def _attn_prompt(q, k, v, km, vm, tri, zs, gs, mc, x, wso, wo, *, n_heads, n_meta):
    b, t, d = q.shape
    tq = ATTN_TILE
    hd = d // n_heads
    tile = pl.BlockSpec((1, tq, d), lambda i, j: (i, j, 0))
    prev_tile = pl.BlockSpec((1, tq, d), lambda i, j: (i, jnp.maximum(j - 1, 0), 0))
    hbm = pl.BlockSpec(memory_space=pl.ANY)

    def const(shape):
        return pl.BlockSpec(shape, lambda i, j: (0,) * len(shape), pipeline_mode=pl.Buffered(1))

    old = pltpu.VMEM((2, tq, d), BF16)
    return pl.pallas_call(
        functools.partial(_attn_prompt_kernel, n_heads=n_heads, n_meta=n_meta),
        grid=(b, t // tq),
        in_specs=[tile, tile, tile, prev_tile, prev_tile, hbm, hbm, const(km.shape), const(vm.shape),
                  const(tri.shape), tile, tile, tile, tile, const(wso.shape), const(wo.shape)],
        out_specs=tile,
        out_shape=jax.ShapeDtypeStruct((b, t, d), F32),
        scratch_shapes=[pltpu.VMEM((n_heads, tq, hd), F32), pltpu.VMEM((n_heads, tq, 1), F32),
                        old, old, pltpu.SemaphoreType.DMA((2, 2))],
        compiler_params=pltpu.CompilerParams(
            dimension_semantics=("arbitrary", "arbitrary"),
            vmem_limit_bytes=V7X_VMEM_LIMIT_BYTES),
        name="attn_prompt",
    )(q, k, v, k, v, k, v, km, vm, tri, zs, gs, mc, x, wso, wo)


def _attn_sample_kernel(qbd_ref, kn_ref, vn_ref, ck_hbm, cv_hbm, tri_ref, zs_ref, gs_ref, mc_ref, x_ref,
                        wso_ref, wo_ref, y_ref, kfirst, vfirst, first_sems, kbuf, vbuf, sems,
                        acc_ref, r_ref, o_ref, *, n_heads, tk):
    b = pl.program_id(0)
    nb = pl.num_programs(0)
    s_new = kn_ref.shape[1]
    d = kn_ref.shape[2]
    hd = d // n_heads
    lanes = qbd_ref.shape[2]
    p_len = ck_hbm.shape[1] // n_heads
    nc = p_len // tk

    def chunk_copies(bb, c, kdst, vdst, sem, slot):
        r0 = pl.multiple_of((p_len - (c + 1) * tk) * n_heads, V7X_SUBLANES)
        rows = pl.ds(r0, tk * n_heads)
        return (pltpu.make_async_copy(ck_hbm.at[bb, rows, :], kdst.at[slot], sem.at[0, slot]),
                pltpu.make_async_copy(cv_hbm.at[bb, rows, :], vdst.at[slot], sem.at[1, slot]))

    def first_copies(bb):
        return chunk_copies(bb, 0, kfirst, vfirst, first_sems, bb & 1)

    def older_copies(c):
        return chunk_copies(b, c, kbuf, vbuf, sems, c & 1)

    @pl.when(b == 0)
    def _():
        for cp in first_copies(b):
            cp.start()

    @pl.when(b + 1 < nb)
    def _():
        for cp in first_copies(b + 1):
            cp.start()

    def block(kblk, vblk, tri, mask):
        z2 = jnp.dot(kblk, qbd_ref[0], preferred_element_type=F32)
        sp = _softplus2(z2)
        spm = sp if mask is None else jnp.where(mask, sp, 0.0)
        suf = jnp.dot(tri, spm.astype(BF16), preferred_element_type=F32)
        r = r_ref[...]
        a = jnp.exp2((z2 - sp) - (suf + r))
        if mask is not None:
            a = jnp.where(mask, a, 0.0)
        acc_ref[...] += lax.dot_general(a.astype(BF16), vblk, (((0,), (0,)), ((), ())),
                                        preferred_element_type=F32)
        r_new = r + jnp.sum(spm, axis=0, keepdims=True)
        r_ref[...] = r_new
        return jnp.min(r_new)

    acc_ref[...] = jnp.zeros_like(acc_ref)
    r_ref[...] = jnp.zeros_like(r_ref)
    key = lax.broadcasted_iota(jnp.int32, (s_new, lanes), 0)
    qi = lax.broadcasted_iota(jnp.int32, (s_new, lanes), 1) % s_new
    rmin = block(kn_ref[0], vn_ref[0], tri_ref[0:s_new, 0:s_new], key < qi)

    def gather_heads(buf, slot):
        return jnp.concatenate(
            [buf[slot, pl.ds(h, tk, stride=n_heads), :] for h in range(n_heads)], axis=-1).astype(BF16)

    for cp in first_copies(b):
        cp.wait()
    rmin = lax.cond(
        rmin < UNDERFLOW_BITS,
        lambda: block(gather_heads(kfirst, b & 1), gather_heads(vfirst, b & 1), tri_ref[...], None),
        lambda: rmin)

    def cond(cr):
        c, rmin = cr
        return jnp.logical_and(c < nc, rmin < UNDERFLOW_BITS)

    @pl.when(cond((1, rmin)))
    def _():
        for cp in older_copies(1):
            cp.start()

    def body(cr):
        c, _ = cr
        for cp in older_copies(c):
            cp.wait()

        @pl.when(c + 1 < nc)
        def _():
            for cp in older_copies(c + 1):
                cp.start()

        return c + 1, block(gather_heads(kbuf, c & 1), gather_heads(vbuf, c & 1), tri_ref[...], None)

    c_end, _ = lax.while_loop(cond, body, (jnp.int32(1), rmin))

    @pl.when(jnp.logical_and(c_end > 1, c_end < nc))
    def _():
        for cp in older_copies(c_end):
            cp.wait()

    acc = acc_ref[...]
    o_ref[b] = jnp.concatenate(
        [acc[h * s_new:(h + 1) * s_new, h * hd:(h + 1) * hd] for h in range(n_heads)], axis=-1)

    @pl.when(b == nb - 1)
    def _():
        rows = nb * s_new
        lead = zs_ref.shape[0] - nb
        flat = lambda ref, skip: ref[skip:].reshape(rows, d)
        y = _merge_out(flat(o_ref, 0), flat(zs_ref, lead), flat(gs_ref, lead), flat(mc_ref, lead),
                       flat(x_ref, 0), wso_ref, wo_ref)
        y_ref[...] = y.reshape(nb, s_new, d)


def _attn_sample(qbd, kn, vn, ck, cv, tri, zs, gs, mc, x, wso, wo, *, n_heads):
    b, s_new, d = x.shape
    lead = kn.shape[0] - b
    hd = d // n_heads
    tk = SAMPLE_KEY_TILE
    lanes = qbd.shape[2]
    per_b = lambda shape, skip=0: pl.BlockSpec((1,) + shape, lambda i: (i + skip, 0, 0))
    whole = lambda a: pl.BlockSpec(a.shape, lambda i: (0, 0, 0), pipeline_mode=pl.Buffered(1))
    hbm = pl.BlockSpec(memory_space=pl.ANY)
    chunk = pltpu.VMEM((2, tk * n_heads, hd), F32)

    def const(shape):
        return pl.BlockSpec(shape, lambda i: (0,) * len(shape), pipeline_mode=pl.Buffered(1))

    return pl.pallas_call(
        functools.partial(_attn_sample_kernel, n_heads=n_heads, tk=tk),
        grid=(b,),
        in_specs=[per_b((d, lanes)), per_b((s_new, d), lead), per_b((s_new, d), lead), hbm, hbm,
                  const(tri.shape), whole(zs), whole(gs), whole(mc), whole(x),
                  const(wso.shape), const(wo.shape)],
        out_specs=pl.BlockSpec((b, s_new, d), lambda i: (0, 0, 0)),
        out_shape=jax.ShapeDtypeStruct((b, s_new, d), F32),
        scratch_shapes=[chunk, chunk, pltpu.SemaphoreType.DMA((2, 2)),
                        chunk, chunk, pltpu.SemaphoreType.DMA((2, 2)),
                        pltpu.VMEM((lanes, d), F32), pltpu.VMEM((1, lanes), F32),
                        pltpu.VMEM((b, s_new, d), F32)],
        compiler_params=pltpu.CompilerParams(
            dimension_semantics=("arbitrary",),
            vmem_limit_bytes=V7X_VMEM_LIMIT_BYTES),
        name="attn_sample",
    )(qbd, kn, vn, ck, cv, tri, zs, gs, mc, x, wso, wo)


def kernel(x_prompt, x_sample, cache_k, cache_v, state_conv, meta_tokens, norm_g, w_in, b_in,
           q_norm_g, k_norm_g, conv_w, w_conv_o, w_sb_o, w_o):
    depth, dec_b, p_len, n_heads, hd = cache_k.shape
    assert depth == 1, "single-layer stack only"
    b, t, d = x_prompt.shape
    s_new = x_sample.shape[1]
    n_meta = meta_tokens.shape[0]
    assert n_heads * hd == d and conv_w.shape[1] == 3 and w_in.shape[2] == N_SPLITS * d
    assert n_meta == s_new, "meta tokens and running streams share one projection call"
    assert t % ATTN_TILE == 0 and t % PROMPT_ROW_TILE == 0 and p_len % SAMPLE_KEY_TILE == 0
    assert n_meta <= META_KEY_PAD

    weights = (norm_g[0][None], w_in[0].astype(BF16), b_in[0][None], q_norm_g[0][None],
               k_norm_g[0][None], conv_w[0], w_conv_o[0].astype(BF16))
    wso = w_sb_o[0].astype(BF16)
    wo = w_o[0].astype(BF16)

    x_ms = jnp.concatenate([meta_tokens[None], x_sample], axis=0)
    prev_ms = jnp.concatenate([jnp.zeros((1, 2, d), F32), state_conv[0]], axis=0)
    ms = _proj(x_ms, prev_ms, *weights, block_b=dec_b + 1, block_t=s_new, n_heads=n_heads)

    prev_p = jnp.broadcast_to(ms["cs"][0:1], (b, 2, d))
    pr = _proj(x_prompt, prev_p, *weights, block_b=1, block_t=PROMPT_ROW_TILE, n_heads=n_heads,
               lead_kv=(ms["k32"][0:1], ms["v32"][0:1]))

    idx = jnp.arange(ATTN_TILE)
    tri_q = (idx[:, None] > idx[None, :]).astype(BF16)
    km = jnp.zeros((META_KEY_PAD, d), BF16).at[:n_meta].set(ms["k"][0])
    vm = jnp.zeros((META_KEY_PAD, d), BF16).at[:n_meta].set(ms["v"][0])
    y_prompt = _attn_prompt(pr["q"], pr["k"], pr["v"], km, vm, tri_q, pr["zs"], pr["gs"], pr["mc"],
                            x_prompt, wso, wo, n_heads=n_heads, n_meta=n_meta)

    idk = jnp.arange(SAMPLE_KEY_TILE)
    tri_k = (idk[None, :] > idk[:, None]).astype(BF16)
    qs = ms["q"][1:].reshape(dec_b, s_new, n_heads, hd)
    eye = jnp.eye(n_heads, dtype=BF16)
    qbd = jnp.einsum("bihd,hg->bhdgi", qs, eye).reshape(dec_b, d, n_heads * s_new)
    ck = cache_k[0].reshape(dec_b, p_len * n_heads, hd)
    cv = cache_v[0].reshape(dec_b, p_len * n_heads, hd)
    y_sample = _attn_sample(qbd, ms["k"], ms["v"], ck, cv, tri_k, ms["zs"], ms["gs"], ms["mc"],
                            x_sample, wso, wo, n_heads=n_heads)

    def heads(a):
        return a.reshape(1, a.shape[0], a.shape[1] // n_heads, n_heads, hd)

    return (y_prompt, y_sample, heads(pr["k32"]), heads(pr["v32"]), pr["cs"][None],
            heads(ms["k32"][1:]), heads(ms["v32"][1:]), ms["cs"][1:][None])
```

```python
import functools
import math

import jax
import jax.numpy as jnp
from jax import lax
from jax.experimental import pallas as pl
from jax.experimental.pallas import tpu as pltpu

EPS = 1e-6
N_SPLITS = 10
UNDERFLOW_BITS = 160.0
EXP2_CLAMP = 64.0
LOG2_E = math.log2(math.e)
V7X_VMEM_LIMIT_BYTES = 60000 * 1024
V7X_SUBLANES = 8
PROMPT_ROW_TILE = 512
ATTN_TILE = 256
ATTN_TILES_PER_STEP = 2
META_KEY_PAD = 128
SAMPLE_KEY_TILE = 256

F32 = jnp.float32
BF16 = jnp.bfloat16


def _softplus2(z2):
    return jnp.maximum(z2, jnp.log(1.0 + jnp.exp2(jnp.minimum(z2, EXP2_CLAMP))) * LOG2_E)


class _KvSlabs:
    def __init__(self, kst_ref, vst_ref, k32_hbm, v32_hbm, sems, *, step, n_steps, n_heads):
        self.pairs = ((kst_ref, k32_hbm), (vst_ref, v32_hbm))
        self.sems, self.step, self.n_steps, self.n_heads = sems, step, n_steps, n_heads
        self.slot = step & 1

    def _copies(self, slot, b0=0, r0=0):
        _, bt, srows, _ = self.pairs[0][0].shape
        return [pltpu.make_async_copy(
            st.at[slot], out.at[pl.ds(b0, bt), pl.ds(r0, srows), :], self.sems.at[i, slot])
            for i, (st, out) in enumerate(self.pairs)]

    def wait_reusable(self):
        @pl.when(self.step >= 2)
        def _():
            for copy in self._copies(self.slot):
                copy.wait()

    def stage(self, which, val):
        st = self.pairs[which][0]
        _, bt, srows, hd = st.shape
        tt = srows // self.n_heads
        for h in range(self.n_heads):
            st[self.slot, :, pl.ds(h, tt, stride=self.n_heads), :] = (
                val[:, h * hd:(h + 1) * hd].reshape(bt, tt, hd))

    def start(self, b0, pos0):
        r0 = pl.multiple_of(pos0 * self.n_heads, V7X_SUBLANES)
        for copy in self._copies(self.slot, b0, r0):
            copy.start()

    def drain_on_last_step(self):
        @pl.when(self.step == self.n_steps - 1)
        def _():
            for copy in self._copies(self.slot):
                copy.wait()

        @pl.when(jnp.logical_and(self.step == self.n_steps - 1, self.n_steps >= 2))
        def _():
            for copy in self._copies(1 - self.slot):
                copy.wait()


def _proj_kernel(*refs, n_heads, n_lead):
    lead_refs, refs = (refs[:2], refs[2:]) if n_lead else ((), refs)
    (x_ref, prev_ref, ng_ref, win_ref, bin_ref, qg_ref, kg_ref, cw_ref, wco_ref,
     q_ref, k_ref, v_ref, k32_hbm, v32_hbm, mc_ref, zs_ref, gs_ref, cs_ref,
     carry_ref, kst_ref, vst_ref, sems, lead_sem) = refs
    bi = pl.program_id(0)
    t = pl.program_id(1)
    nt = pl.num_programs(1)
    bt, tt, d = x_ref.shape
    rows = bt * tt
    hd = d // n_heads
    slabs = _KvSlabs(kst_ref, vst_ref, k32_hbm, v32_hbm, sems, step=bi * nt + t,
                     n_steps=pl.num_programs(0) * nt, n_heads=n_heads)
    slabs.wait_reusable()

    if n_lead:
        @pl.when(t == 0)
        def _():
            for lead_ref, out_hbm in zip(lead_refs, (k32_hbm, v32_hbm)):
                dst = out_hbm.at[pl.ds(bi * bt, bt), pl.ds(0, n_lead * n_heads), :]
                copy = pltpu.make_async_copy(lead_ref, dst, lead_sem.at[0])
                copy.start()
                copy.wait()

    x = x_ref[...].reshape(rows, d)
    xn = x * lax.rsqrt(jnp.mean(x * x, axis=-1, keepdims=True) + EPS) * ng_ref[...]
    xn = xn.astype(BF16)

    def proj(i):
        w = win_ref[:, i * d:(i + 1) * d]
        return jnp.dot(xn, w, preferred_element_type=F32) + bin_ref[:, i * d:(i + 1) * d]

    u = proj(1) * proj(2)

    @pl.when(t == 0)
    def _():
        carry_ref[...] = prev_ref[...]

    prev = carry_ref[...]
    p0 = prev[:, 0:1, :]
    p1 = prev[:, 1:2, :]
    u3 = u.reshape(bt, tt, d)
    tidx = lax.broadcasted_iota(jnp.int32, (bt, tt, d), 1)
    r1 = pltpu.roll(u, 1, 0).reshape(bt, tt, d)
    r2 = pltpu.roll(u, 2, 0).reshape(bt, tt, d)
    um1 = jnp.where(tidx == 0, p1, r1)
    um2 = jnp.where(tidx == 0, p0, jnp.where(tidx == 1, p1, r2))
    cw = cw_ref[...]
    conv = cw[0:1, :] * um2 + cw[1:2, :] * um1 + cw[2:3, :] * u3
    tail = u3[:, tt - 2:tt, :]
    carry_ref[...] = tail
    cs_ref[...] = tail

    yc = (jax.nn.silu(proj(3)) * (proj(0) * conv.reshape(rows, d))).astype(BF16)
    gc = jax.nn.sigmoid(proj(8))

    def head_norm(p, g_ref, scale):
        outs = []
        for h in range(n_heads):
            ph = p[:, h * hd:(h + 1) * hd]
            n = ph * lax.rsqrt(jnp.mean(ph * ph, axis=-1, keepdims=True) + EPS) * g_ref[...]
            outs.append(n * scale if scale is not None else n)
        return jnp.concatenate(outs, axis=-1)

    qn = head_norm(proj(4), qg_ref, LOG2_E * hd ** -0.5)
    q_ref[...] = qn.astype(BF16).reshape(bt, tt, d)
    kn = head_norm(proj(5), kg_ref, None)
    slabs.stage(0, kn)
    k_ref[...] = kn.astype(BF16).reshape(bt, tt, d)
    vv = proj(6)
    slabs.stage(1, vv)
    v_ref[...] = vv.astype(BF16).reshape(bt, tt, d)
    zs_ref[...] = jax.nn.silu(proj(7)).astype(BF16).reshape(bt, tt, d)
    gs_ref[...] = jax.nn.sigmoid(proj(9)).astype(BF16).reshape(bt, tt, d)
    mc = gc * jnp.dot(yc, wco_ref[...], preferred_element_type=F32)
    mc_ref[...] = mc.astype(BF16).reshape(bt, tt, d)

    slabs.start(bi * bt, n_lead + t * tt)
    slabs.drain_on_last_step()


def _proj(x, prev, ng, win, bin_, qg, kg, cw, wco, *, block_b, block_t, n_heads, lead_kv=()):
    b, t, d = x.shape
    hd = d // n_heads
    n_lead = lead_kv[0].shape[1] // n_heads if lead_kv else 0
    assert not lead_kv or block_b == 1
    grid = (b // block_b, t // block_t)
    row_spec = pl.BlockSpec((block_b, block_t, d), lambda i, j: (i, j, 0))
    state_spec = pl.BlockSpec((block_b, 2, d), lambda i, j: (i, 0, 0))
    hbm = pl.BlockSpec(memory_space=pl.ANY)

    def const(shape):
        return pl.BlockSpec(shape, lambda i, j: (0,) * len(shape), pipeline_mode=pl.Buffered(1))

    bf = jax.ShapeDtypeStruct((b, t, d), BF16)
    kv32 =jax.ShapeDtypeStruct((b, (n_lead + t) * n_heads, hd), F32)
    stage = pltpu.VMEM((2, block_b, block_t * n_heads, hd), F32)
    outs = pl.pallas_call(
        functools.partial(_proj_kernel, n_heads=n_heads, n_lead=n_lead),
        grid=grid,
        in_specs=[const(a.shape) for a in lead_kv]
        + [row_spec, state_spec, const((1, d)), const(win.shape), const(bin_.shape),
           const(qg.shape), const(kg.shape), const(cw.shape), const(wco.shape)],
        out_specs=[row_spec] * 3 + [hbm, hbm] + [row_spec] * 3 + [state_spec],
        out_shape=[bf, bf, bf, kv32, kv32, bf, bf, bf, jax.ShapeDtypeStruct((b, 2, d), F32)],
        scratch_shapes=[pltpu.VMEM((block_b, 2, d), F32), stage, stage,
                        pltpu.SemaphoreType.DMA((2, 2)), pltpu.SemaphoreType.DMA((1,))],
        compiler_params=pltpu.CompilerParams(
            dimension_semantics=("arbitrary", "arbitrary"),
            vmem_limit_bytes=V7X_VMEM_LIMIT_BYTES),
        name="proj",
    )(*lead_kv, x, prev, ng, win, bin_, qg, kg, cw, wco)
    names = ("q", "k", "v", "k32", "v32", "mc", "zs", "gs", "cs")
    return dict(zip(names, outs))


def _merge_out(o, zs, gs, mc, x, wso_ref, wo_ref):
    ys = (zs.astype(F32) * o).astype(BF16)
    m = mc.astype(F32) + gs.astype(F32) * jnp.dot(ys, wso_ref[...], preferred_element_type=F32)
    return x + jnp.dot(m.astype(BF16), wo_ref[...], preferred_element_type=F32)


def _attn_prompt_kernel(q_ref, kcur_ref, vcur_ref, kprev_ref, vprev_ref, k_hbm, v_hbm, km_ref, vm_ref,
                        tri_ref, zs_ref, gs_ref, mc_ref, x_ref, wso_ref, wo_ref, y_ref,
                        acc_ref, r_ref, kbuf, vbuf, sems, *, n_heads, n_meta):
    tq = kprev_ref.shape[1]
    n_sub = q_ref.shape[1] // tq
    for s in range(n_sub):
        rows = slice(s * tq, (s + 1) * tq)
        if s == 0:
            prev_kv = (kprev_ref[0], vprev_ref[0])
            prev_mask = jnp.broadcast_to(pl.program_id(1) >= 1, (tq, tq))
        else:
            prev_kv = (kcur_ref[0, (s - 1) * tq:s * tq, :], vcur_ref[0, (s - 1) * tq:s * tq, :])
            prev_mask = None
        _attn_prompt_tile(
            pl.program_id(1) * n_sub + s, q_ref.at[0, rows, :],
            (kcur_ref[0, rows, :], vcur_ref[0, rows, :]), prev_kv, prev_mask, k_hbm, v_hbm, km_ref, vm_ref,
            tri_ref, zs_ref.at[0, rows, :], gs_ref.at[0, rows, :], mc_ref.at[0, rows, :], x_ref.at[0, rows, :],
            wso_ref, wo_ref, y_ref.at[0, rows, :], acc_ref, r_ref, kbuf, vbuf, sems,
            n_heads=n_heads, n_meta=n_meta)


def _attn_prompt_tile(t, q_ref, cur_kv, prev_kv, prev_mask, k_hbm, v_hbm, km_ref, vm_ref, tri_ref,
                      zs_ref, gs_ref, mc_ref, x_ref, wso_ref, wo_ref, y_ref, acc_ref, r_ref, kbuf, vbuf, sems,
                      *, n_heads, n_meta):
    bi = pl.program_id(0)
    tq, d = q_ref.shape
    hd = d // n_heads

    acc_ref[...] = jnp.zeros_like(acc_ref)
    r_ref[...] = jnp.zeros_like(r_ref)

    def block(kblk, vblk, tri, mask):
        logb, spms = [], []
        for h in range(n_heads):
            hs = slice(h * hd, (h + 1) * hd)
            z2 = lax.dot_general(q_ref[:, hs], kblk[:, hs], (((1,), (1,)), ((), ())),
                                 preferred_element_type=F32)
            sp = _softplus2(z2)
            logb.append(z2 - sp)
            spms.append(sp if mask is None else jnp.where(mask, sp, 0.0))
        suf_all = jnp.dot(jnp.concatenate([s.astype(BF16) for s in spms], axis=0), tri,
                          preferred_element_type=F32)
        r_low = None
        for h in range(n_heads):
            hs = slice(h * hd, (h + 1) * hd)
            r = r_ref[h]
            a = jnp.exp2(logb[h] - (suf_all[h * tq:(h + 1) * tq] + r))
            if mask is not None:
                a = jnp.where(mask, a, 0.0)
            acc_ref[h] += jnp.dot(a.astype(BF16), vblk[:, hs], preferred_element_type=F32)
            r_new = r + jnp.sum(spms[h], axis=-1, keepdims=True)
            r_ref[h] = r_new
            r_low = r_new if r_low is None else jnp.minimum(r_low, r_new)
        return jnp.min(r_low)

    row = lax.broadcasted_iota(jnp.int32, (tq, tq), 0)
    col = lax.broadcasted_iota(jnp.int32, (tq, tq), 1)
    block(*cur_kv, tri_ref[...], col < row)
    rmin = block(*prev_kv, tri_ref[...], prev_mask)

    def old_copies(c, slot):
        rows = pl.ds(pl.multiple_of((t - 2 - c) * tq, tq), tq)
        return (pltpu.make_async_copy(k_hbm.at[bi, rows, :], kbuf.at[slot], sems.at[0, slot]),
                pltpu.make_async_copy(v_hbm.at[bi, rows, :], vbuf.at[slot], sems.at[1, slot]))

    n_old = jnp.maximum(t - 1, 0)

    @pl.when(jnp.logical_and(n_old > 0, rmin < UNDERFLOW_BITS))
    def _():
        for copy in old_copies(0, 0):
            copy.start()

    def cond(cr):
        c, rmin = cr
        return jnp.logical_and(c < n_old, rmin < UNDERFLOW_BITS)

    def body(cr):
        c, _ = cr
        slot = c & 1
        for copy in old_copies(c, slot):
            copy.wait()

        @pl.when(c + 1 < n_old)
        def _():
            for copy in old_copies(c + 1, 1 - slot):
                copy.start()

        return c + 1, block(kbuf[slot], vbuf[slot], tri_ref[...], None)

    c_end, rmin = lax.while_loop(cond, body, (jnp.int32(0), rmin))

    @pl.when(jnp.logical_and(c_end > 0, c_end < n_old))
    def _():
        for copy in old_copies(c_end, c_end & 1):
            copy.wait()

    @pl.when(rmin < UNDERFLOW_BITS)
    def _():
        wm = km_ref.shape[0]
        mcol = lax.broadcasted_iota(jnp.int32, (tq, wm), 1)
        block(km_ref[...], vm_ref[...], tri_ref[0:wm, 0:wm], mcol < n_meta)

    o = jnp.concatenate([acc_ref[h] for h in range(n_heads)], axis=-1)
    y_ref[...] = _merge_out(o, zs_ref[...], gs_ref[...], mc_ref[...], x_ref[...], wso_ref, wo_ref)


def _attn_prompt(q, k, v, km, vm, tri, zs, gs, mc, x, wso, wo, *, n_heads, n_meta):
    b, t, d = q.shape
    tq = ATTN_TILE
    n_sub = ATTN_TILES_PER_STEP
    hd = d // n_heads
    tile = pl.BlockSpec((1, n_sub * tq, d), lambda i, j: (i, j, 0))
    prev_tile = pl.BlockSpec((1, tq, d), lambda i, j: (i, jnp.maximum(n_sub * j - 1, 0), 0))
    hbm = pl.BlockSpec(memory_space=pl.ANY)

    def const(shape):
        return pl.BlockSpec(shape, lambda i, j: (0,) * len(shape), pipeline_mode=pl.Buffered(1))

    old = pltpu.VMEM((2, tq, d), BF16)
    return pl.pallas_call(
        functools.partial(_attn_prompt_kernel, n_heads=n_heads, n_meta=n_meta),
        grid=(b, t // (n_sub * tq)),
        in_specs=[tile, tile, tile, prev_tile, prev_tile, hbm, hbm, const(km.shape), const(vm.shape),
                  const(tri.shape), tile, tile, tile, tile, const(wso.shape), const(wo.shape)],
        out_specs=tile,
        out_shape=jax.ShapeDtypeStruct((b, t, d), F32),
        scratch_shapes=[pltpu.VMEM((n_heads, tq, hd), F32), pltpu.VMEM((n_heads, tq, 1), F32),
                        old, old, pltpu.SemaphoreType.DMA((2, 2))],
        compiler_params=pltpu.CompilerParams(
            dimension_semantics=("arbitrary", "arbitrary"),
            vmem_limit_bytes=V7X_VMEM_LIMIT_BYTES),
        name="attn_prompt",
    )(q, k, v, k, v, k, v, km, vm, tri, zs, gs, mc, x, wso, wo)


def _attn_sample_kernel(qbd_ref, kn_ref, vn_ref, ck_hbm, cv_hbm, tri_ref, zs_ref, gs_ref, mc_ref, x_ref,
                        wso_ref, wo_ref, y_ref, kfirst, vfirst, first_sems, kbuf, vbuf, sems,
                        acc_ref, r_ref, o_ref, *, n_heads, tk):
    b = pl.program_id(0)
    nb = pl.num_programs(0)
    s_new = kn_ref.shape[1]
    d = kn_ref.shape[2]
    hd = d // n_heads
    lanes = qbd_ref.shape[2]
    p_len = ck_hbm.shape[1] // n_heads
    nc = p_len // tk

    def chunk_copies(bb, c, kdst, vdst, sem, slot):
        r0 = pl.multiple_of((p_len - (c + 1) * tk) * n_heads, V7X_SUBLANES)
        rows = pl.ds(r0, tk * n_heads)
        return (pltpu.make_async_copy(ck_hbm.at[bb, rows, :], kdst.at[slot], sem.at[0, slot]),
                pltpu.make_async_copy(cv_hbm.at[bb, rows, :], vdst.at[slot], sem.at[1, slot]))

    def first_copies(bb):
        return chunk_copies(bb, 0, kfirst, vfirst, first_sems, bb & 1)

    def older_copies(c):
        return chunk_copies(b, c, kbuf, vbuf, sems, c & 1)

    @pl.when(b == 0)
    def _():
        for cp in first_copies(b):
            cp.start()

    @pl.when(b + 1 < nb)
    def _():
        for cp in first_copies(b + 1):
            cp.start()

    def block(kblk, vblk, tri, mask):
        z2 = jnp.dot(kblk, qbd_ref[0], preferred_element_type=F32)
        sp = _softplus2(z2)
        spm = sp if mask is None else jnp.where(mask, sp, 0.0)
        suf = jnp.dot(tri, spm.astype(BF16), preferred_element_type=F32)
        r = r_ref[...]
        a = jnp.exp2((z2 - sp) - (suf + r))
        if mask is not None:
            a = jnp.where(mask, a, 0.0)
        acc_ref[...] += lax.dot_general(a.astype(BF16), vblk, (((0,), (0,)), ((), ())),
                                        preferred_element_type=F32)
        r_new = r + jnp.sum(spm, axis=0, keepdims=True)
        r_ref[...] = r_new
        return jnp.min(r_new)

    acc_ref[...] = jnp.zeros_like(acc_ref)
    r_ref[...] = jnp.zeros_like(r_ref)
    key = lax.broadcasted_iota(jnp.int32, (s_new, lanes), 0)
    qi = lax.broadcasted_iota(jnp.int32, (s_new, lanes), 1) % s_new
    rmin = block(kn_ref[0], vn_ref[0], tri_ref[0:s_new, 0:s_new], key < qi)

    def gather_heads(buf, slot):
        return jnp.concatenate(
            [buf[slot, pl.ds(h, tk, stride=n_heads), :] for h in range(n_heads)], axis=-1).astype(BF16)

    for cp in first_copies(b):
        cp.wait()
    rmin = lax.cond(
        rmin < UNDERFLOW_BITS,
        lambda: block(gather_heads(kfirst, b & 1), gather_heads(vfirst, b & 1), tri_ref[...], None),
        lambda: rmin)

    def cond(cr):
        c, rmin = cr
        return jnp.logical_and(c < nc, rmin < UNDERFLOW_BITS)

    @pl.when(cond((1, rmin)))
    def _():
        for cp in older_copies(1):
            cp.start()

    def body(cr):
        c, _ = cr
        for cp in older_copies(c):
            cp.wait()

        @pl.when(c + 1 < nc)
        def _():
            for cp in older_copies(c + 1):
                cp.start()

        return c + 1, block(gather_heads(kbuf, c & 1), gather_heads(vbuf, c & 1), tri_ref[...], None)

    c_end, _ = lax.while_loop(cond, body, (jnp.int32(1), rmin))

    @pl.when(jnp.logical_and(c_end > 1, c_end < nc))
    def _():
        for cp in older_copies(c_end):
            cp.wait()

    acc = acc_ref[...]
    o_ref[b] = jnp.concatenate(
        [acc[h * s_new:(h + 1) * s_new, h * hd:(h + 1) * hd] for h in range(n_heads)], axis=-1)

    @pl.when(b == nb - 1)
    def _():
        rows = nb * s_new
        lead = zs_ref.shape[0] - nb
        flat = lambda ref, skip: ref[skip:].reshape(rows, d)
        y = _merge_out(flat(o_ref, 0), flat(zs_ref, lead), flat(gs_ref, lead), flat(mc_ref, lead),
                       flat(x_ref, 0), wso_ref, wo_ref)
        y_ref[...] = y.reshape(nb, s_new, d)


def _attn_sample(qbd, kn, vn, ck, cv, tri, zs, gs, mc, x, wso, wo, *, n_heads):
    b, s_new, d = x.shape
    lead = kn.shape[0] - b
    hd = d // n_heads
    tk = SAMPLE_KEY_TILE
    lanes = qbd.shape[2]
    per_b = lambda shape, skip=0: pl.BlockSpec((1,) + shape, lambda i: (i + skip, 0, 0))
    whole = lambda a: pl.BlockSpec(a.shape, lambda i: (0, 0, 0), pipeline_mode=pl.Buffered(1))
    hbm = pl.BlockSpec(memory_space=pl.ANY)
    chunk = pltpu.VMEM((2, tk * n_heads, hd), F32)

    def const(shape):
        return pl.BlockSpec(shape, lambda i: (0,) * len(shape), pipeline_mode=pl.Buffered(1))

    return pl.pallas_call(
        functools.partial(_attn_sample_kernel, n_heads=n_heads, tk=tk),
        grid=(b,),
        in_specs=[per_b((d, lanes)), per_b((s_new, d), lead), per_b((s_new, d), lead), hbm, hbm,
                  const(tri.shape), whole(zs), whole(gs), whole(mc), whole(x),
                  const(wso.shape), const(wo.shape)],
        out_specs=pl.BlockSpec((b, s_new, d), lambda i: (0, 0, 0)),
        out_shape=jax.ShapeDtypeStruct((b, s_new, d), F32),
        scratch_shapes=[chunk, chunk, pltpu.SemaphoreType.DMA((2, 2)),
                        chunk, chunk, pltpu.SemaphoreType.DMA((2, 2)),
                        pltpu.VMEM((lanes, d), F32), pltpu.VMEM((1, lanes), F32),
                        pltpu.VMEM((b, s_new, d), F32)],
        compiler_params=pltpu.CompilerParams(
            dimension_semantics=("arbitrary",),
            vmem_limit_bytes=V7X_VMEM_LIMIT_BYTES),
        name="attn_sample",
    )(qbd, kn, vn, ck, cv, tri, zs, gs, mc, x, wso, wo)


def kernel(x_prompt, x_sample, cache_k, cache_v, state_conv, meta_tokens, norm_g, w_in, b_in,
           q_norm_g, k_norm_g, conv_w, w_conv_o, w_sb_o, w_o):
    depth, dec_b, p_len, n_heads, hd = cache_k.shape
    assert depth == 1, "single-layer stack only"
    b, t, d = x_prompt.shape
    s_new = x_sample.shape[1]
    n_meta = meta_tokens.shape[0]
    assert n_heads * hd == d and conv_w.shape[1] == 3 and w_in.shape[2] == N_SPLITS * d
    assert n_meta == s_new, "meta tokens and running streams share one projection call"
    assert t % (ATTN_TILE * ATTN_TILES_PER_STEP) == 0 and t % PROMPT_ROW_TILE == 0
    assert p_len % SAMPLE_KEY_TILE == 0
    assert n_meta <= META_KEY_PAD

    weights = (norm_g[0][None], w_in[0].astype(BF16), b_in[0][None], q_norm_g[0][None],
               k_norm_g[0][None], conv_w[0], w_conv_o[0].astype(BF16))
    wso = w_sb_o[0].astype(BF16)
    wo = w_o[0].astype(BF16)

    x_ms = jnp.concatenate([meta_tokens[None], x_sample], axis=0)
    prev_ms = jnp.concatenate([jnp.zeros((1, 2, d), F32), state_conv[0]], axis=0)
    ms = _proj(x_ms, prev_ms, *weights, block_b=dec_b + 1, block_t=s_new, n_heads=n_heads)

    prev_p = jnp.broadcast_to(ms["cs"][0:1], (b, 2, d))
    pr = _proj(x_prompt, prev_p, *weights, block_b=1, block_t=PROMPT_ROW_TILE, n_heads=n_heads,
               lead_kv=(ms["k32"][0:1], ms["v32"][0:1]))

    idx = jnp.arange(ATTN_TILE)
    tri_q = (idx[:, None] > idx[None, :]).astype(BF16)
    km = jnp.zeros((META_KEY_PAD, d), BF16).at[:n_meta].set(ms["k"][0])
    vm = jnp.zeros((META_KEY_PAD, d), BF16).at[:n_meta].set(ms["v"][0])
    y_prompt = _attn_prompt(pr["q"], pr["k"], pr["v"], km, vm, tri_q, pr["zs"], pr["gs"], pr["mc"],
                            x_prompt, wso, wo, n_heads=n_heads, n_meta=n_meta)

    idk = jnp.arange(SAMPLE_KEY_TILE)
    tri_k = (idk[None, :] > idk[:, None]).astype(BF16)
    qs = ms["q"][1:].reshape(dec_b, s_new, n_heads, hd)
    eye = jnp.eye(n_heads, dtype=BF16)
    qbd = jnp.einsum("bihd,hg->bhdgi", qs, eye).reshape(dec_b, d, n_heads * s_new)
    ck = cache_k[0].reshape(dec_b, p_len * n_heads, hd)
    cv = cache_v[0].reshape(dec_b, p_len * n_heads, hd)
    y_sample = _attn_sample(qbd, ms["k"], ms["v"], ck, cv, tri_k, ms["zs"], ms["gs"], ms["mc"],
                            x_sample, wso, wo, n_heads=n_heads)

    def heads(a):
        return a.reshape(1, a.shape[0], a.shape[1] // n_heads, n_heads, hd)

    return (y_prompt, y_sample, heads(pr["k32"]), heads(pr["v32"]), pr["cs"][None],
            heads(ms["k32"][1:]), heads(ms["v32"][1:]), ms["cs"][1:][None])
```

```python
import functools
import math

import jax
import jax.numpy as jnp
from jax import lax
from jax.experimental import pallas as pl
from jax.experimental.pallas import tpu as pltpu

EPS = 1e-6
N_SPLITS = 10
UNDERFLOW_BITS = 160.0
EXP2_CLAMP = 64.0
LOG2_E = math.log2(math.e)
V7X_VMEM_LIMIT_BYTES = 60000 * 1024
V7X_SUBLANES = 8
PROMPT_ROW_TILE = 512
ATTN_TILE = 256
ATTN_TILES_PER_STEP = 2
META_KEY_PAD = 128
SAMPLE_KEY_TILE = 256

F32 = jnp.float32
BF16 = jnp.bfloat16


def _softplus2(z2):
    return jnp.maximum(z2, jnp.log(1.0 + jnp.exp2(jnp.minimum(z2, EXP2_CLAMP))) * LOG2_E)


class _KvSlabs:
    def __init__(self, kst_ref, vst_ref, k32_hbm, v32_hbm, sems, *, step, n_steps, n_heads):
        self.pairs = ((kst_ref, k32_hbm), (vst_ref, v32_hbm))
        self.sems, self.step, self.n_steps, self.n_heads = sems, step, n_steps, n_heads
        self.slot = step & 1

    def _copies(self, slot, b0=0, r0=0):
        _, bt, srows, _ = self.pairs[0][0].shape
        return [pltpu.make_async_copy(
            st.at[slot], out.at[pl.ds(b0, bt), pl.ds(r0, srows), :], self.sems.at[i, slot])
            for i, (st, out) in enumerate(self.pairs)]

    def wait_reusable(self):
        @pl.when(self.step >= 2)
        def _():
            for copy in self._copies(self.slot):
                copy.wait()

    def stage(self, which, val):
        st = self.pairs[which][0]
        _, bt, srows, hd = st.shape
        tt = srows // self.n_heads
        for h in range(self.n_heads):
            st[self.slot, :, pl.ds(h, tt, stride=self.n_heads), :] = (
                val[:, h * hd:(h + 1) * hd].reshape(bt, tt, hd))

    def start(self, b0, pos0):
        r0 = pl.multiple_of(pos0 * self.n_heads, V7X_SUBLANES)
        for copy in self._copies(self.slot, b0, r0):
            copy.start()

    def drain_on_last_step(self):
        @pl.when(self.step == self.n_steps - 1)
        def _():
            for copy in self._copies(self.slot):
                copy.wait()

        @pl.when(jnp.logical_and(self.step == self.n_steps - 1, self.n_steps >= 2))
        def _():
            for copy in self._copies(1 - self.slot):
                copy.wait()


PROJ_ORDER = (1, 2, 3, 0, 8, 4, 5, 6, 7, 9)


class _WeightCaster:
    def __init__(self, sources, dests, stage_ref, bf_ref, in_sems, out_sems):
        self.sources, self.dests = sources, dests
        self.stage_ref, self.bf_ref, self.in_sems, self.out_sems = stage_ref, bf_ref, in_sems, out_sems
        self.taken = 0
        self._fetch(0).start()

    def _fetch(self, k):
        return pltpu.make_async_copy(self.sources[k], self.stage_ref.at[k % 2], self.in_sems.at[k % 2])

    def _emit(self, k):
        return pltpu.make_async_copy(self.bf_ref.at[k], self.dests[k], self.out_sems.at[k])

    def take(self):
        k = self.taken
        self.taken += 1
        self._fetch(k).wait()
        if k + 1 < len(self.sources):
            self._fetch(k + 1).start()
        self.bf_ref[k] = self.stage_ref[k % 2].astype(BF16)
        self._emit(k).start()
        return self.bf_ref[k]

    def finish(self):
        assert self.taken == len(self.sources)
        for k in range(len(self.sources)):
            self._emit(k).wait()


def _proj_kernel(*refs, n_heads, n_lead, cast_weights):
    lead_refs, refs = (refs[:2], refs[2:]) if n_lead else ((), refs)
    if cast_weights:
        (x_ref, prev_ref, ng_ref, win_ref, bin_ref, qg_ref, kg_ref, cw_ref, wco_ref, wso_hbm, wo_hbm,
         q_ref, k_ref, v_ref, k32_hbm, v32_hbm, mc_ref, zs_ref, gs_ref, cs_ref,
         winbf_hbm, wcobf_hbm, wsobf_hbm, wobf_hbm,
         carry_ref, kst_ref, vst_ref, sems, lead_sem, wstage_ref, wbf_ref, win_sems, wout_sems) = refs
    else:
        (x_ref, prev_ref, ng_ref, win_ref, bin_ref, qg_ref, kg_ref, cw_ref, wco_ref,
         q_ref, k_ref, v_ref, k32_hbm, v32_hbm, mc_ref, zs_ref, gs_ref, cs_ref,
         carry_ref, kst_ref, vst_ref, sems, lead_sem) = refs
    bi = pl.program_id(0)
    t = pl.program_id(1)
    nt = pl.num_programs(1)
    bt, tt, d = x_ref.shape
    rows = bt * tt
    hd = d // n_heads
    slabs = _KvSlabs(kst_ref, vst_ref, k32_hbm, v32_hbm, sems, step=bi * nt + t,
                     n_steps=pl.num_programs(0) * nt, n_heads=n_heads)
    slabs.wait_reusable()

    if cast_weights:
        col = lambda ref, i: ref.at[:, pl.ds(i * d, d)]
        caster = _WeightCaster(
            [col(win_ref, i) for i in PROJ_ORDER] + [wco_ref, wso_hbm, wo_hbm],
            [col(winbf_hbm, i) for i in PROJ_ORDER] + [wcobf_hbm, wsobf_hbm, wobf_hbm],
            wstage_ref, wbf_ref, win_sems, wout_sems)

    if n_lead:
        @pl.when(t == 0)
        def _():
            for lead_ref, out_hbm in zip(lead_refs, (k32_hbm, v32_hbm)):
                dst = out_hbm.at[pl.ds(bi * bt, bt), pl.ds(0, n_lead * n_heads), :]
                copy = pltpu.make_async_copy(lead_ref, dst, lead_sem.at[0])
                copy.start()
                copy.wait()

    x = x_ref[...].reshape(rows, d)
    xn = x * lax.rsqrt(jnp.mean(x * x, axis=-1, keepdims=True) + EPS) * ng_ref[...]
    xn = xn.astype(BF16)

    def proj(i):
        if cast_weights:
            assert PROJ_ORDER[caster.taken] == i
            w = caster.take()
        else:
            w = win_ref[:, i * d:(i + 1) * d]
        return jnp.dot(xn, w, preferred_element_type=F32) + bin_ref[:, i * d:(i + 1) * d]

    u = proj(1) * proj(2)

    @pl.when(t == 0)
    def _():
        carry_ref[...] = prev_ref[...]

    prev = carry_ref[...]
    p0 = prev[:, 0:1, :]
    p1 = prev[:, 1:2, :]
    u3 = u.reshape(bt, tt, d)
    tidx = lax.broadcasted_iota(jnp.int32, (bt, tt, d), 1)
    r1 = pltpu.roll(u, 1, 0).reshape(bt, tt, d)
    r2 = pltpu.roll(u, 2, 0).reshape(bt, tt, d)
    um1 = jnp.where(tidx == 0, p1, r1)
    um2 = jnp.where(tidx == 0, p0, jnp.where(tidx == 1, p1, r2))
    cw = cw_ref[...]
    conv = cw[0:1, :] * um2 + cw[1:2, :] * um1 + cw[2:3, :] * u3
    tail = u3[:, tt - 2:tt, :]
    carry_ref[...] = tail
    cs_ref[...] = tail

    yc = (jax.nn.silu(proj(3)) * (proj(0) * conv.reshape(rows, d))).astype(BF16)
    gc = jax.nn.sigmoid(proj(8))

    def head_norm(p, g_ref, scale):
        outs = []
        for h in range(n_heads):
            ph = p[:, h * hd:(h + 1) * hd]
            n = ph * lax.rsqrt(jnp.mean(ph * ph, axis=-1, keepdims=True) + EPS) * g_ref[...]
            outs.append(n * scale if scale is not None else n)
        return jnp.concatenate(outs, axis=-1)

    qn = head_norm(proj(4), qg_ref, LOG2_E * hd ** -0.5)
    q_ref[...] = qn.astype(BF16).reshape(bt, tt, d)
    kn = head_norm(proj(5), kg_ref, None)
    slabs.stage(0, kn)
    k_ref[...] = kn.astype(BF16).reshape(bt, tt, d)
    vv = proj(6)
    slabs.stage(1, vv)
    v_ref[...] = vv.astype(BF16).reshape(bt, tt, d)
    zs_ref[...] = jax.nn.silu(proj(7)).astype(BF16).reshape(bt, tt, d)
    gs_ref[...] = jax.nn.sigmoid(proj(9)).astype(BF16).reshape(bt, tt, d)
    wco = caster.take() if cast_weights else wco_ref[...]
    mc = gc * jnp.dot(yc, wco, preferred_element_type=F32)
    mc_ref[...] = mc.astype(BF16).reshape(bt, tt, d)

    slabs.start(bi * bt, n_lead + t * tt)
    slabs.drain_on_last_step()
    if cast_weights:
        caster.take()
        caster.take()
        caster.finish()


def _proj(x, prev, ng, win, bin_, qg, kg, cw, wco, *, block_b, block_t, n_heads, lead_kv=(), merge_w=()):
    b, t, d = x.shape
    hd = d // n_heads
    n_lead = lead_kv[0].shape[1] // n_heads if lead_kv else 0
    assert not lead_kv or block_b == 1
    grid = (b // block_b, t // block_t)
    cast_weights = bool(merge_w)
    assert not cast_weights or grid == (1, 1)
    row_spec = pl.BlockSpec((block_b, block_t, d), lambda i, j: (i, j, 0))
    state_spec = pl.BlockSpec((block_b, 2, d), lambda i, j: (i, 0, 0))
    hbm = pl.BlockSpec(memory_space=pl.ANY)

    def const(shape):
        return pl.BlockSpec(shape, lambda i, j: (0,) * len(shape), pipeline_mode=pl.Buffered(1))

    w_spec = (lambda a: hbm) if cast_weights else (lambda a: const(a.shape))
    bf = jax.ShapeDtypeStruct((b, t, d), BF16)
    kv32 = jax.ShapeDtypeStruct((b, (n_lead + t) * n_heads, hd), F32)
    stage = pltpu.VMEM((2, block_b, block_t * n_heads, hd), F32)
    weights_out = [jax.ShapeDtypeStruct(a.shape, BF16) for a in (win, wco, *merge_w)] if cast_weights else []
    n_blocks = len(PROJ_ORDER) + 1 + len(merge_w)
    cast_scratch = [pltpu.VMEM((2, d, d), F32), pltpu.VMEM((n_blocks, d, d), BF16),
                    pltpu.SemaphoreType.DMA((2,)), pltpu.SemaphoreType.DMA((n_blocks,))] if cast_weights else []
    outs = pl.pallas_call(
        functools.partial(_proj_kernel, n_heads=n_heads, n_lead=n_lead, cast_weights=cast_weights),
        grid=grid,
        in_specs=[const(a.shape) for a in lead_kv]
        + [row_spec, state_spec, const((1, d)), w_spec(win), const(bin_.shape),
           const(qg.shape), const(kg.shape), const(cw.shape), w_spec(wco)] + [hbm for _ in merge_w],
        out_specs=[row_spec] * 3 + [hbm, hbm] + [row_spec] * 3 + [state_spec] + [hbm for _ in weights_out],
        out_shape=[bf, bf, bf, kv32, kv32, bf, bf, bf, jax.ShapeDtypeStruct((b, 2, d), F32)] + weights_out,
        scratch_shapes=[pltpu.VMEM((block_b, 2, d), F32), stage, stage,
                        pltpu.SemaphoreType.DMA((2, 2)), pltpu.SemaphoreType.DMA((1,))] + cast_scratch,
        compiler_params=pltpu.CompilerParams(
            dimension_semantics=("arbitrary", "arbitrary"),
            vmem_limit_bytes=V7X_VMEM_LIMIT_BYTES),
        name="proj",
    )(*lead_kv, x, prev, ng, win, bin_, qg, kg, cw, wco, *merge_w)
    names = ("q", "k", "v", "k32", "v32", "mc", "zs", "gs", "cs", "win", "wco", "wso", "wo")
    return dict(zip(names, outs))


def _merge_out(o, zs, gs, mc, x, wso_ref, wo_ref):
    ys = (zs.astype(F32) * o).astype(BF16)
    m = mc.astype(F32) + gs.astype(F32) * jnp.dot(ys, wso_ref[...], preferred_element_type=F32)
    return x + jnp.dot(m.astype(BF16), wo_ref[...], preferred_element_type=F32)


def _attn_prompt_kernel(q_ref, kcur_ref, vcur_ref, kprev_ref, vprev_ref, k_hbm, v_hbm, km_ref, vm_ref,
                        tri_ref, zs_ref, gs_ref, mc_ref, x_ref, wso_ref, wo_ref, y_ref,
                        acc_ref, r_ref, kbuf, vbuf, sems, *, n_heads, n_meta):
    tq = kprev_ref.shape[1]
    n_sub = q_ref.shape[1] // tq
    tiles = [_PromptTile(pl.program_id(1) * n_sub + s, q_ref.at[0, s * tq:(s + 1) * tq, :], tri_ref,
                         acc_ref.at[s], r_ref.at[s], n_heads) for s in range(n_sub)]
    for s, tile in enumerate(tiles):
        rows = slice(s * tq, (s + 1) * tq)
        if s == 0:
            prev_kv = (kprev_ref[0], vprev_ref[0])
            prev_mask = jnp.broadcast_to(pl.program_id(1) >= 1, (tq, tq))
        else:
            prev_kv = (kcur_ref[0, (s - 1) * tq:s * tq, :], vcur_ref[0, (s - 1) * tq:s * tq, :])
            prev_mask = None
        tile.fold_newest((kcur_ref[0, rows, :], vcur_ref[0, rows, :]), prev_kv, prev_mask)
    for tile in tiles:
        tile.fold_older(k_hbm, v_hbm, kbuf, vbuf, sems, km_ref, vm_ref, n_meta)
    for s, tile in enumerate(tiles):
        rows = slice(s * tq, (s + 1) * tq)
        y_ref[0, rows, :] = _merge_out(tile.output(), zs_ref[0, rows, :], gs_ref[0, rows, :],
                                       mc_ref[0, rows, :], x_ref[0, rows, :], wso_ref, wo_ref)


class _PromptTile:
    def __init__(self, t, q_ref, tri_ref, acc_ref, r_ref, n_heads):
        self.t, self.q_ref, self.tri_ref, self.acc_ref, self.r_ref = t, q_ref, tri_ref, acc_ref, r_ref
        self.n_heads = n_heads
        self.tq, d = q_ref.shape
        self.hd = d // n_heads
        self.rmin = None

    def block(self, kblk, vblk, tri, mask):
        tq, hd, q_ref, acc_ref, r_ref = self.tq, self.hd, self.q_ref, self.acc_ref, self.r_ref
        logb, spms = [], []
        for h in range(self.n_heads):
            hs = slice(h * hd, (h + 1) * hd)
            z2 = lax.dot_general(q_ref[:, hs], kblk[:, hs], (((1,), (1,)), ((), ())),
                                 preferred_element_type=F32)
            sp = _softplus2(z2)
            logb.append(z2 - sp)
            spms.append(sp if mask is None else jnp.where(mask, sp, 0.0))
        suf_all = jnp.dot(jnp.concatenate([s.astype(BF16) for s in spms], axis=0), tri,
                          preferred_element_type=F32)
        r_low = None
        for h in range(self.n_heads):
            hs = slice(h * hd, (h + 1) * hd)
            r = r_ref[h]
            a = jnp.exp2(logb[h] - (suf_all[h * tq:(h + 1) * tq] + r))
            if mask is not None:
                a = jnp.where(mask, a, 0.0)
            acc_ref[h] += jnp.dot(a.astype(BF16), vblk[:, hs], preferred_element_type=F32)
            r_new = r + jnp.sum(spms[h], axis=-1, keepdims=True)
            r_ref[h] = r_new
            r_low = r_new if r_low is None else jnp.minimum(r_low, r_new)
        return jnp.min(r_low)

    def fold_newest(self, cur_kv, prev_kv, prev_mask):
        tq = self.tq
        self.acc_ref[...] = jnp.zeros_like(self.acc_ref)
        self.r_ref[...] = jnp.zeros_like(self.r_ref)
        row = lax.broadcasted_iota(jnp.int32, (tq, tq), 0)
        col = lax.broadcasted_iota(jnp.int32, (tq, tq), 1)
        self.block(*cur_kv, self.tri_ref[...], col < row)
        self.rmin = self.block(*prev_kv, self.tri_ref[...], prev_mask)

    def fold_older(self, k_hbm, v_hbm, kbuf, vbuf, sems, km_ref, vm_ref, n_meta):
        t, tq, tri_ref = self.t, self.tq, self.tri_ref
        bi = pl.program_id(0)

        def old_copies(c, slot):
            rows = pl.ds(pl.multiple_of((t - 2 - c) * tq, tq), tq)
            return (pltpu.make_async_copy(k_hbm.at[bi, rows, :], kbuf.at[slot], sems.at[0, slot]),
                    pltpu.make_async_copy(v_hbm.at[bi, rows, :], vbuf.at[slot], sems.at[1, slot]))

        n_old = jnp.maximum(t - 1, 0)

        @pl.when(jnp.logical_and(n_old > 0, self.rmin < UNDERFLOW_BITS))
        def _():
            for copy in old_copies(0, 0):
                copy.start()

        def cond(cr):
            c, rmin = cr
            return jnp.logical_and(c < n_old, rmin < UNDERFLOW_BITS)

        def body(cr):
            c, _ = cr
            slot = c & 1
            for copy in old_copies(c, slot):
                copy.wait()

            @pl.when(c + 1 < n_old)
            def _():
                for copy in old_copies(c + 1, 1 - slot):
                    copy.start()

            return c + 1, self.block(kbuf[slot], vbuf[slot], tri_ref[...], None)

        c_end, rmin = lax.while_loop(cond, body, (jnp.int32(0), self.rmin))

        @pl.when(jnp.logical_and(c_end > 0, c_end < n_old))
        def _():
            for copy in old_copies(c_end, c_end & 1):
                copy.wait()

        @pl.when(rmin < UNDERFLOW_BITS)
        def _():
            wm = km_ref.shape[0]
            mcol = lax.broadcasted_iota(jnp.int32, (tq, wm), 1)
            self.block(km_ref[...], vm_ref[...], tri_ref[0:wm, 0:wm], mcol < n_meta)

    def output(self):
        return jnp.concatenate([self.acc_ref[h] for h in range(self.n_heads)], axis=-1)


def _attn_prompt(q, k, v, km, vm, tri, zs, gs, mc, x, wso, wo, *, n_heads, n_meta):
    b, t, d = q.shape
    tq = ATTN_TILE
    n_sub = ATTN_TILES_PER_STEP
    hd = d // n_heads
    tile = pl.BlockSpec((1, n_sub * tq, d), lambda i, j: (i, j, 0))
    prev_tile = pl.BlockSpec((1, tq, d), lambda i, j: (i, jnp.maximum(n_sub * j - 1, 0), 0))
    hbm = pl.BlockSpec(memory_space=pl.ANY)

    def const(shape):
        return pl.BlockSpec(shape, lambda i, j: (0,) * len(shape), pipeline_mode=pl.Buffered(1))

    old = pltpu.VMEM((2, tq, d), BF16)
    return pl.pallas_call(
        functools.partial(_attn_prompt_kernel, n_heads=n_heads, n_meta=n_meta),
        grid=(b, t // (n_sub * tq)),
        in_specs=[tile, tile, tile, prev_tile, prev_tile, hbm, hbm, const(km.shape), const(vm.shape),
                  const(tri.shape), tile, tile, tile, tile, const(wso.shape), const(wo.shape)],
        out_specs=tile,
        out_shape=jax.ShapeDtypeStruct((b, t, d), F32),
        scratch_shapes=[pltpu.VMEM((n_sub, n_heads, tq, hd), F32), pltpu.VMEM((n_sub, n_heads, tq, 1), F32),
                        old, old, pltpu.SemaphoreType.DMA((2, 2))],
        compiler_params=pltpu.CompilerParams(
            dimension_semantics=("arbitrary", "arbitrary"),
            vmem_limit_bytes=V7X_VMEM_LIMIT_BYTES),
        name="attn_prompt",
    )(q, k, v, k, v, k, v, km, vm, tri, zs, gs, mc, x, wso, wo)


def _attn_sample_kernel(q_ref, kn_ref, vn_ref, ck_hbm, cv_hbm, tri_ref, zs_ref, gs_ref, mc_ref, x_ref,
                        wso_ref, wo_ref, y_ref, kfirst, vfirst, first_sems, kbuf, vbuf, sems,
                        acc_ref, r_ref, o_ref, *, n_heads, tk):
    b = pl.program_id(0)
    nb = pl.num_programs(0)
    s_new = kn_ref.shape[1]
    d = kn_ref.shape[2]
    hd = d // n_heads
    lanes = n_heads * s_new
    p_len = ck_hbm.shape[1] // n_heads
    nc = p_len // tk

    query = lax.broadcasted_iota(jnp.int32, (s_new, lanes), 0)
    repeat = (lax.broadcasted_iota(jnp.int32, (s_new, lanes), 1) % s_new == query).astype(BF16)
    q_rep = lax.dot_general(q_ref[0], repeat, (((0,), (0,)), ((), ())), preferred_element_type=F32)
    same_head = (lax.broadcasted_iota(jnp.int32, (d, lanes), 0) // hd
                 == lax.broadcasted_iota(jnp.int32, (d, lanes), 1) // s_new)
    qbd = jnp.where(same_head, q_rep, 0.0).astype(BF16)

    def chunk_copies(bb, c, kdst, vdst, sem, slot):
        r0 = pl.multiple_of((p_len - (c + 1) * tk) * n_heads, V7X_SUBLANES)
        rows = pl.ds(r0, tk * n_heads)
        return (pltpu.make_async_copy(ck_hbm.at[bb, rows, :], kdst.at[slot], sem.at[0, slot]),
                pltpu.make_async_copy(cv_hbm.at[bb, rows, :], vdst.at[slot], sem.at[1, slot]))

    def first_copies(bb):
        return chunk_copies(bb, 0, kfirst, vfirst, first_sems, bb & 1)

    def older_copies(c):
        return chunk_copies(b, c, kbuf, vbuf, sems, c & 1)

    @pl.when(b == 0)
    def _():
        for cp in first_copies(b):
            cp.start()

    @pl.when(b + 1 < nb)
    def _():
        for cp in first_copies(b + 1):
            cp.start()

    def block(kblk, vblk, tri, mask):
        z2 = jnp.dot(kblk, qbd, preferred_element_type=F32)
        sp = _softplus2(z2)
        spm = sp if mask is None else jnp.where(mask, sp, 0.0)
        suf = jnp.dot(tri, spm.astype(BF16), preferred_element_type=F32)
        r = r_ref[...]
        a = jnp.exp2((z2 - sp) - (suf + r))
        if mask is not None:
            a = jnp.where(mask, a, 0.0)
        acc_ref[...] += lax.dot_general(a.astype(BF16), vblk, (((0,), (0,)), ((), ())),
                                        preferred_element_type=F32)
        r_new = r + jnp.sum(spm, axis=0, keepdims=True)
        r_ref[...] = r_new
        return jnp.min(r_new)

    acc_ref[...] = jnp.zeros_like(acc_ref)
    r_ref[...] = jnp.zeros_like(r_ref)
    key = lax.broadcasted_iota(jnp.int32, (s_new, lanes), 0)
    qi = lax.broadcasted_iota(jnp.int32, (s_new, lanes), 1) % s_new
    rmin = block(kn_ref[0], vn_ref[0], tri_ref[0:s_new, 0:s_new], key < qi)

    def gather_heads(buf, slot):
        return jnp.concatenate(
            [buf[slot, pl.ds(h, tk, stride=n_heads), :] for h in range(n_heads)], axis=-1).astype(BF16)

    for cp in first_copies(b):
        cp.wait()
    rmin = lax.cond(
        rmin < UNDERFLOW_BITS,
        lambda: block(gather_heads(kfirst, b & 1), gather_heads(vfirst, b & 1), tri_ref[...], None),
        lambda: rmin)

    def cond(cr):
        c, rmin = cr
        return jnp.logical_and(c < nc, rmin < UNDERFLOW_BITS)

    @pl.when(cond((1, rmin)))
    def _():
        for cp in older_copies(1):
            cp.start()

    def body(cr):
        c, _ = cr
        for cp in older_copies(c):
            cp.wait()

        @pl.when(c + 1 < nc)
        def _():
            for cp in older_copies(c + 1):
                cp.start()

        return c + 1, block(gather_heads(kbuf, c & 1), gather_heads(vbuf, c & 1), tri_ref[...], None)

    c_end, _ = lax.while_loop(cond, body, (jnp.int32(1), rmin))

    @pl.when(jnp.logical_and(c_end > 1, c_end < nc))
    def _():
        for cp in older_copies(c_end):
            cp.wait()

    acc = acc_ref[...]
    o_ref[b] = jnp.concatenate(
        [acc[h * s_new:(h + 1) * s_new, h * hd:(h + 1) * hd] for h in range(n_heads)], axis=-1)

    @pl.when(b == nb - 1)
    def _():
        rows = nb * s_new
        lead = zs_ref.shape[0] - nb
        flat = lambda ref, skip: ref[skip:].reshape(rows, d)
        y = _merge_out(flat(o_ref, 0), flat(zs_ref, lead), flat(gs_ref, lead), flat(mc_ref, lead),
                       flat(x_ref, 0), wso_ref, wo_ref)
        y_ref[...] = y.reshape(nb, s_new, d)


def _attn_sample(q, kn, vn, ck, cv, tri, zs, gs, mc, x, wso, wo, *, n_heads):
    b, s_new, d = x.shape
    lead = kn.shape[0] - b
    hd = d // n_heads
    tk = SAMPLE_KEY_TILE
    lanes = n_heads * s_new
    per_b = lambda shape, skip=0: pl.BlockSpec((1,) + shape, lambda i: (i + skip, 0, 0))
    whole = lambda a: pl.BlockSpec(a.shape, lambda i: (0, 0, 0), pipeline_mode=pl.Buffered(1))
    hbm = pl.BlockSpec(memory_space=pl.ANY)
    chunk = pltpu.VMEM((2, tk * n_heads, hd), F32)

    def const(shape):
        return pl.BlockSpec(shape, lambda i: (0,) * len(shape), pipeline_mode=pl.Buffered(1))

    return pl.pallas_call(
        functools.partial(_attn_sample_kernel, n_heads=n_heads, tk=tk),
        grid=(b,),
        in_specs=[per_b((s_new, d), lead), per_b((s_new, d), lead), per_b((s_new, d), lead), hbm, hbm,
                  const(tri.shape), whole(zs), whole(gs), whole(mc), whole(x),
                  const(wso.shape), const(wo.shape)],
        out_specs=pl.BlockSpec((b, s_new, d), lambda i: (0, 0, 0)),
        out_shape=jax.ShapeDtypeStruct((b, s_new, d), F32),
        scratch_shapes=[chunk, chunk, pltpu.SemaphoreType.DMA((2, 2)),
                        chunk, chunk, pltpu.SemaphoreType.DMA((2, 2)),
                        pltpu.VMEM((lanes, d), F32), pltpu.VMEM((1, lanes), F32),
                        pltpu.VMEM((b, s_new, d), F32)],
        compiler_params=pltpu.CompilerParams(
            dimension_semantics=("arbitrary",),
            vmem_limit_bytes=V7X_VMEM_LIMIT_BYTES),
        name="attn_sample",
    )(q, kn, vn, ck, cv, tri, zs, gs, mc, x, wso, wo)


def kernel(x_prompt, x_sample, cache_k, cache_v, state_conv, meta_tokens, norm_g, w_in, b_in,
           q_norm_g, k_norm_g, conv_w, w_conv_o, w_sb_o, w_o):
    depth, dec_b, p_len, n_heads, hd = cache_k.shape
    assert depth == 1, "single-layer stack only"
    b, t, d = x_prompt.shape
    s_new = x_sample.shape[1]
    n_meta = meta_tokens.shape[0]
    assert n_heads * hd == d and conv_w.shape[1] == 3 and w_in.shape[2] == N_SPLITS * d
    assert n_meta == s_new, "meta tokens and running streams share one projection call"
    assert t % (ATTN_TILE * ATTN_TILES_PER_STEP) == 0 and t % PROMPT_ROW_TILE == 0
    assert p_len % SAMPLE_KEY_TILE == 0
    assert n_meta <= META_KEY_PAD

    small = (norm_g[0][None], b_in[0][None], q_norm_g[0][None], k_norm_g[0][None], conv_w[0])
    ng, bin_, qg, kg, cw = small

    x_ms = jnp.concatenate([meta_tokens[None], x_sample], axis=0)
    prev_ms = jnp.concatenate([jnp.zeros((1, 2, d), F32), state_conv[0]], axis=0)
    ms = _proj(x_ms, prev_ms, ng, w_in[0], bin_, qg, kg, cw, w_conv_o[0], block_b=dec_b + 1, block_t=s_new,
               n_heads=n_heads, merge_w=(w_sb_o[0], w_o[0]))
    wso, wo = ms["wso"], ms["wo"]

    prev_p = jnp.broadcast_to(ms["cs"][0:1], (b, 2, d))
    pr = _proj(x_prompt, prev_p, ng, ms["win"], bin_, qg, kg, cw, ms["wco"], block_b=1,
               block_t=PROMPT_ROW_TILE, n_heads=n_heads, lead_kv=(ms["k32"][0:1], ms["v32"][0:1]))

    idx = jnp.arange(ATTN_TILE)
    tri_q = (idx[:, None] > idx[None, :]).astype(BF16)
    km = jnp.zeros((META_KEY_PAD, d), BF16).at[:n_meta].set(ms["k"][0])
    vm = jnp.zeros((META_KEY_PAD, d), BF16).at[:n_meta].set(ms["v"][0])
    y_prompt = _attn_prompt(pr["q"], pr["k"], pr["v"], km, vm, tri_q, pr["zs"], pr["gs"], pr["mc"],
                            x_prompt, wso, wo, n_heads=n_heads, n_meta=n_meta)

    idk = jnp.arange(SAMPLE_KEY_TILE)
    tri_k = (idk[None, :] > idk[:, None]).astype(BF16)
    ck = cache_k[0].reshape(dec_b, p_len * n_heads, hd)
    cv = cache_v[0].reshape(dec_b, p_len * n_heads, hd)
    y_sample = _attn_sample(ms["q"], ms["k"], ms["v"], ck, cv, tri_k, ms["zs"], ms["gs"], ms["mc"],
                            x_sample, wso, wo, n_heads=n_heads)

    def heads(a):
        return a.reshape(1, a.shape[0], a.shape[1] // n_heads, n_heads, hd)

    return (y_prompt, y_sample, heads(pr["k32"]), heads(pr["v32"]), pr["cs"][None],
            heads(ms["k32"][1:]), heads(ms["v32"][1:]), ms["cs"][1:][None])
```

```python
import functools
import math

import jax
import jax.numpy as jnp
from jax import lax
from jax.experimental import pallas as pl
from jax.experimental.pallas import tpu as pltpu

EPS = 1e-6
N_SPLITS = 10
UNDERFLOW_BITS = 160.0
EXP2_CLAMP = 64.0
LOG2_E = math.log2(math.e)
V7X_VMEM_LIMIT_BYTES = 60000 * 1024
V7X_SUBLANES = 8
PROMPT_ROW_TILE = 512
ATTN_TILE = 256
ATTN_TILES_PER_STEP = 2
META_KEY_PAD = 128
SAMPLE_KEY_TILE = 256
WEIGHT_FETCH_SLOTS = 4

F32 = jnp.float32
BF16 = jnp.bfloat16


def _softplus2(z2):
    return jnp.maximum(z2, jnp.log(1.0 + jnp.exp2(jnp.minimum(z2, EXP2_CLAMP))) * LOG2_E)


class _KvSlabs:
    def __init__(self, kst_ref, vst_ref, k32_hbm, v32_hbm, sems, *, step, n_steps, n_heads):
        self.pairs = ((kst_ref, k32_hbm), (vst_ref, v32_hbm))
        self.sems, self.step, self.n_steps, self.n_heads = sems, step, n_steps, n_heads
        self.slot = step & 1

    def _copies(self, slot, b0=0, r0=0):
        _, bt, srows, _ = self.pairs[0][0].shape
        return [pltpu.make_async_copy(
            st.at[slot], out.at[pl.ds(b0, bt), pl.ds(r0, srows), :], self.sems.at[i, slot])
            for i, (st, out) in enumerate(self.pairs)]

    def wait_reusable(self):
        @pl.when(self.step >= 2)
        def _():
            for copy in self._copies(self.slot):
                copy.wait()

    def stage(self, which, val):
        st = self.pairs[which][0]
        _, bt, srows, hd = st.shape
        tt = srows // self.n_heads
        for h in range(self.n_heads):
            st[self.slot, :, pl.ds(h, tt, stride=self.n_heads), :] = (
                val[:, h * hd:(h + 1) * hd].reshape(bt, tt, hd))

    def start(self, b0, pos0):
        r0 = pl.multiple_of(pos0 * self.n_heads, V7X_SUBLANES)
        for copy in self._copies(self.slot, b0, r0):
            copy.start()

    def drain_on_last_step(self):
        @pl.when(self.step == self.n_steps - 1)
        def _():
            for copy in self._copies(self.slot):
                copy.wait()

        @pl.when(jnp.logical_and(self.step == self.n_steps - 1, self.n_steps >= 2))
        def _():
            for copy in self._copies(1 - self.slot):
                copy.wait()


PROJ_ORDER = (1, 2, 3, 0, 8, 4, 5, 6, 7, 9)


class _WeightCaster:
    def __init__(self, sources, dests, stage_ref, bf_ref, in_sems, out_sems):
        self.sources, self.dests = sources, dests
        self.stage_ref, self.bf_ref, self.in_sems, self.out_sems = stage_ref, bf_ref, in_sems, out_sems
        self.depth = stage_ref.shape[0]
        self.taken = 0
        for k in range(min(self.depth - 1, len(sources))):
            self._fetch(k).start()

    def _fetch(self, k):
        slot = k % self.depth
        return pltpu.make_async_copy(self.sources[k], self.stage_ref.at[slot], self.in_sems.at[slot])

    def _emit(self, k):
        return pltpu.make_async_copy(self.bf_ref.at[k], self.dests[k], self.out_sems.at[k])

    def take(self):
        k = self.taken
        self.taken += 1
        self._fetch(k).wait()
        ahead = k + self.depth - 1
        if ahead < len(self.sources):
            self._fetch(ahead).start()
        self.bf_ref[k] = self.stage_ref[k % self.depth].astype(BF16)
        self._emit(k).start()
        return self.bf_ref[k]

    def finish(self):
        assert self.taken == len(self.sources)
        for k in range(len(self.sources)):
            self._emit(k).wait()


def _proj_kernel(*refs, n_heads, n_lead, cast_weights):
    lead_refs, refs = (refs[:2], refs[2:]) if n_lead else ((), refs)
    if cast_weights:
        (x_ref, prev_ref, ng_ref, win_ref, bin_ref, qg_ref, kg_ref, cw_ref, wco_ref, wso_hbm, wo_hbm,
         q_ref, k_ref, v_ref, k32_hbm, v32_hbm, mc_ref, zs_ref, gs_ref, cs_ref,
         winbf_hbm, wcobf_hbm, wsobf_hbm, wobf_hbm,
         carry_ref, kst_ref, vst_ref, sems, lead_sem, wstage_ref, wbf_ref, win_sems, wout_sems) = refs
    else:
        (x_ref, prev_ref, ng_ref, win_ref, bin_ref, qg_ref, kg_ref, cw_ref, wco_ref,
         q_ref, k_ref, v_ref, k32_hbm, v32_hbm, mc_ref, zs_ref, gs_ref, cs_ref,
         carry_ref, kst_ref, vst_ref, sems, lead_sem) = refs
    bi = pl.program_id(0)
    t = pl.program_id(1)
    nt = pl.num_programs(1)
    bt, tt, d = x_ref.shape
    rows = bt * tt
    hd = d // n_heads
    slabs = _KvSlabs(kst_ref, vst_ref, k32_hbm, v32_hbm, sems, step=bi * nt + t,
                     n_steps=pl.num_programs(0) * nt, n_heads=n_heads)
    slabs.wait_reusable()

    if cast_weights:
        col = lambda ref, i: ref.at[:, pl.ds(i * d, d)]
        caster = _WeightCaster(
            [col(win_ref, i) for i in PROJ_ORDER] + [wco_ref, wso_hbm, wo_hbm],
            [col(winbf_hbm, i) for i in PROJ_ORDER] + [wcobf_hbm, wsobf_hbm, wobf_hbm],
            wstage_ref, wbf_ref, win_sems, wout_sems)

    if n_lead:
        @pl.when(t == 0)
        def _():
            for lead_ref, out_hbm in zip(lead_refs, (k32_hbm, v32_hbm)):
                dst = out_hbm.at[pl.ds(bi * bt, bt), pl.ds(0, n_lead * n_heads), :]
                copy = pltpu.make_async_copy(lead_ref, dst, lead_sem.at[0])
                copy.start()
                copy.wait()

    x = x_ref[...].reshape(rows, d)
    xn = x * lax.rsqrt(jnp.mean(x * x, axis=-1, keepdims=True) + EPS) * ng_ref[...]
    xn = xn.astype(BF16)

    def proj(i):
        if cast_weights:
            assert PROJ_ORDER[caster.taken] == i
            w = caster.take()
        else:
            w = win_ref[:, i * d:(i + 1) * d]
        return jnp.dot(xn, w, preferred_element_type=F32) + bin_ref[:, i * d:(i + 1) * d]

    u = proj(1) * proj(2)

    @pl.when(t == 0)
    def _():
        carry_ref[...] = prev_ref[...]

    prev = carry_ref[...]
    p0 = prev[:, 0:1, :]
    p1 = prev[:, 1:2, :]
    u3 = u.reshape(bt, tt, d)
    tidx = lax.broadcasted_iota(jnp.int32, (bt, tt, d), 1)
    r1 = pltpu.roll(u, 1, 0).reshape(bt, tt, d)
    r2 = pltpu.roll(u, 2, 0).reshape(bt, tt, d)
    um1 = jnp.where(tidx == 0, p1, r1)
    um2 = jnp.where(tidx == 0, p0, jnp.where(tidx == 1, p1, r2))
    cw = cw_ref[...]
    conv = cw[0:1, :] * um2 + cw[1:2, :] * um1 + cw[2:3, :] * u3
    tail = u3[:, tt - 2:tt, :]
    carry_ref[...] = tail
    cs_ref[...] = tail

    yc = (jax.nn.silu(proj(3)) * (proj(0) * conv.reshape(rows, d))).astype(BF16)
    gc = jax.nn.sigmoid(proj(8))

    def head_norm(p, g_ref, scale):
        outs = []
        for h in range(n_heads):
            ph = p[:, h * hd:(h + 1) * hd]
            n = ph * lax.rsqrt(jnp.mean(ph * ph, axis=-1, keepdims=True) + EPS) * g_ref[...]
            outs.append(n * scale if scale is not None else n)
        return jnp.concatenate(outs, axis=-1)

    qn = head_norm(proj(4), qg_ref, LOG2_E * hd ** -0.5)
    q_ref[...] = qn.astype(BF16).reshape(bt, tt, d)
    kn = head_norm(proj(5), kg_ref, None)
    slabs.stage(0, kn)
    k_ref[...] = kn.astype(BF16).reshape(bt, tt, d)
    vv = proj(6)
    slabs.stage(1, vv)
    v_ref[...] = vv.astype(BF16).reshape(bt, tt, d)
    zs_ref[...] = jax.nn.silu(proj(7)).astype(BF16).reshape(bt, tt, d)
    gs_ref[...] = jax.nn.sigmoid(proj(9)).astype(BF16).reshape(bt, tt, d)
    wco = caster.take() if cast_weights else wco_ref[...]
    mc = gc * jnp.dot(yc, wco, preferred_element_type=F32)
    mc_ref[...] = mc.astype(BF16).reshape(bt, tt, d)

    slabs.start(bi * bt, n_lead + t * tt)
    slabs.drain_on_last_step()
    if cast_weights:
        caster.take()
        caster.take()
        caster.finish()


def _proj(x, prev, ng, win, bin_, qg, kg, cw, wco, *, block_b, block_t, n_heads, lead_kv=(), merge_w=(),
          shared_prev=False):
    b, t, d = x.shape
    hd = d // n_heads
    n_lead = lead_kv[0].shape[1] // n_heads if lead_kv else 0
    assert not lead_kv or block_b == 1
    grid = (b // block_b, t // block_t)
    cast_weights = bool(merge_w)
    assert not cast_weights or grid == (1, 1)
    row_spec = pl.BlockSpec((block_b, block_t, d), lambda i, j: (i, j, 0))
    state_spec = pl.BlockSpec((block_b, 2, d), lambda i, j: (i, 0, 0))
    hbm = pl.BlockSpec(memory_space=pl.ANY)

    def const(shape):
        return pl.BlockSpec(shape, lambda i, j: (0,) * len(shape), pipeline_mode=pl.Buffered(1))

    stream0 = lambda a: const((1,) + a.shape[1:])
    prev_spec = stream0(prev) if shared_prev else state_spec

    w_spec = (lambda a: hbm) if cast_weights else (lambda a: const(a.shape))
    bf = jax.ShapeDtypeStruct((b, t, d), BF16)
    kv32 = jax.ShapeDtypeStruct((b, (n_lead + t) * n_heads, hd), F32)
    stage = pltpu.VMEM((2, block_b, block_t * n_heads, hd), F32)
    weights_out = [jax.ShapeDtypeStruct(a.shape, BF16) for a in (win, wco, *merge_w)] if cast_weights else []
    n_blocks = len(PROJ_ORDER) + 1 + len(merge_w)
    cast_scratch = [pltpu.VMEM((WEIGHT_FETCH_SLOTS, d, d), F32), pltpu.VMEM((n_blocks, d, d), BF16),
                    pltpu.SemaphoreType.DMA((WEIGHT_FETCH_SLOTS,)),
                    pltpu.SemaphoreType.DMA((n_blocks,))] if cast_weights else []
    outs = pl.pallas_call(
        functools.partial(_proj_kernel, n_heads=n_heads, n_lead=n_lead, cast_weights=cast_weights),
        grid=grid,
        in_specs=[stream0(a) for a in lead_kv]
        + [row_spec, prev_spec, const((1, d)), w_spec(win), const(bin_.shape),
           const(qg.shape), const(kg.shape), const(cw.shape), w_spec(wco)] + [hbm for _ in merge_w],
        out_specs=[row_spec] * 3 + [hbm, hbm] + [row_spec] * 3 + [state_spec] + [hbm for _ in weights_out],
        out_shape=[bf, bf, bf, kv32, kv32, bf, bf, bf, jax.ShapeDtypeStruct((b, 2, d), F32)] + weights_out,
        scratch_shapes=[pltpu.VMEM((block_b, 2, d), F32), stage, stage,
                        pltpu.SemaphoreType.DMA((2, 2)), pltpu.SemaphoreType.DMA((1,))] + cast_scratch,
        compiler_params=pltpu.CompilerParams(
            dimension_semantics=("arbitrary", "arbitrary"),
            vmem_limit_bytes=V7X_VMEM_LIMIT_BYTES),
        name="proj",
    )(*lead_kv, x, prev, ng, win, bin_, qg, kg, cw, wco, *merge_w)
    names = ("q", "k", "v", "k32", "v32", "mc", "zs", "gs", "cs", "win", "wco", "wso", "wo")
    return dict(zip(names, outs))


def _merge_out(o, zs, gs, mc, x, wso_ref, wo_ref):
    ys = (zs.astype(F32) * o).astype(BF16)
    m = mc.astype(F32) + gs.astype(F32) * jnp.dot(ys, wso_ref[...], preferred_element_type=F32)
    return x + jnp.dot(m.astype(BF16), wo_ref[...], preferred_element_type=F32)


def _attn_prompt_kernel(q_ref, kcur_ref, vcur_ref, kprev_ref, vprev_ref, k_hbm, v_hbm, km_ref, vm_ref,
                        tri_ref, zs_ref, gs_ref, mc_ref, x_ref, wso_ref, wo_ref, y_ref,
                        acc_ref, r_ref, kbuf, vbuf, sems, kmeta, vmeta, *, n_heads):
    tq = kprev_ref.shape[1]
    n_sub = q_ref.shape[1] // tq
    n_meta = km_ref.shape[1]

    @pl.when(jnp.logical_and(pl.program_id(0) == 0, pl.program_id(1) == 0))
    def _():
        for src, dst in ((km_ref, kmeta), (vm_ref, vmeta)):
            dst[...] = jnp.zeros_like(dst)
            dst[0:n_meta, :] = src[0]

    tiles = [_PromptTile(pl.program_id(1) * n_sub + s, q_ref.at[0, s * tq:(s + 1) * tq, :], tri_ref,
                         acc_ref.at[s], r_ref.at[s], n_heads) for s in range(n_sub)]
    for s, tile in enumerate(tiles):
        rows = slice(s * tq, (s + 1) * tq)
        if s == 0:
            prev_kv = (kprev_ref[0], vprev_ref[0])
            prev_mask = jnp.broadcast_to(pl.program_id(1) >= 1, (tq, tq))
        else:
            prev_kv = (kcur_ref[0, (s - 1) * tq:s * tq, :], vcur_ref[0, (s - 1) * tq:s * tq, :])
            prev_mask = None
        tile.fold_newest((kcur_ref[0, rows, :], vcur_ref[0, rows, :]), prev_kv, prev_mask)
    for tile in tiles:
        tile.fold_older(k_hbm, v_hbm, kbuf, vbuf, sems, kmeta, vmeta, n_meta)
    for s, tile in enumerate(tiles):
        rows = slice(s * tq, (s + 1) * tq)
        y_ref[0, rows, :] = _merge_out(tile.output(), zs_ref[0, rows, :], gs_ref[0, rows, :],
                                       mc_ref[0, rows, :], x_ref[0, rows, :], wso_ref, wo_ref)


class _PromptTile:
    def __init__(self, t, q_ref, tri_ref, acc_ref, r_ref, n_heads):
        self.t, self.q_ref, self.tri_ref, self.acc_ref, self.r_ref = t, q_ref, tri_ref, acc_ref, r_ref
        self.n_heads = n_heads
        self.tq, d = q_ref.shape
        self.hd = d // n_heads
        self.rmin = None

    def block(self, kblk, vblk, tri, mask):
        tq, hd, q_ref, acc_ref, r_ref = self.tq, self.hd, self.q_ref, self.acc_ref, self.r_ref
        logb, spms = [], []
        for h in range(self.n_heads):
            hs = slice(h * hd, (h + 1) * hd)
            z2 = lax.dot_general(q_ref[:, hs], kblk[:, hs], (((1,), (1,)), ((), ())),
                                 preferred_element_type=F32)
            sp = _softplus2(z2)
            logb.append(z2 - sp)
            spms.append(sp if mask is None else jnp.where(mask, sp, 0.0))
        suf_all = jnp.dot(jnp.concatenate([s.astype(BF16) for s in spms], axis=0), tri,
                          preferred_element_type=F32)
        r_low = None
        for h in range(self.n_heads):
            hs = slice(h * hd, (h + 1) * hd)
            r = r_ref[h]
            a = jnp.exp2(logb[h] - (suf_all[h * tq:(h + 1) * tq] + r))
            if mask is not None:
                a = jnp.where(mask, a, 0.0)
            acc_ref[h] += jnp.dot(a.astype(BF16), vblk[:, hs], preferred_element_type=F32)
            r_new = r + jnp.sum(spms[h], axis=-1, keepdims=True)
            r_ref[h] = r_new
            r_low = r_new if r_low is None else jnp.minimum(r_low, r_new)
        return jnp.min(r_low)

    def fold_newest(self, cur_kv, prev_kv, prev_mask):
        tq = self.tq
        self.acc_ref[...] = jnp.zeros_like(self.acc_ref)
        self.r_ref[...] = jnp.zeros_like(self.r_ref)
        row = lax.broadcasted_iota(jnp.int32, (tq, tq), 0)
        col = lax.broadcasted_iota(jnp.int32, (tq, tq), 1)
        self.block(*cur_kv, self.tri_ref[...], col < row)
        self.rmin = self.block(*prev_kv, self.tri_ref[...], prev_mask)

    def fold_older(self, k_hbm, v_hbm, kbuf, vbuf, sems, km_ref, vm_ref, n_meta):
        t, tq, tri_ref = self.t, self.tq, self.tri_ref
        bi = pl.program_id(0)

        def old_copies(c, slot):
            rows = pl.ds(pl.multiple_of((t - 2 - c) * tq, tq), tq)
            return (pltpu.make_async_copy(k_hbm.at[bi, rows, :], kbuf.at[slot], sems.at[0, slot]),
                    pltpu.make_async_copy(v_hbm.at[bi, rows, :], vbuf.at[slot], sems.at[1, slot]))

        n_old = jnp.maximum(t - 1, 0)

        @pl.when(jnp.logical_and(n_old > 0, self.rmin < UNDERFLOW_BITS))
        def _():
            for copy in old_copies(0, 0):
                copy.start()

        def cond(cr):
            c, rmin = cr
            return jnp.logical_and(c < n_old, rmin < UNDERFLOW_BITS)

        def body(cr):
            c, _ = cr
            slot = c & 1
            for copy in old_copies(c, slot):
                copy.wait()

            @pl.when(c + 1 < n_old)
            def _():
                for copy in old_copies(c + 1, 1 - slot):
                    copy.start()

            return c + 1, self.block(kbuf[slot], vbuf[slot], tri_ref[...], None)

        c_end, rmin = lax.while_loop(cond, body, (jnp.int32(0), self.rmin))

        @pl.when(jnp.logical_and(c_end > 0, c_end < n_old))
        def _():
            for copy in old_copies(c_end, c_end & 1):
                copy.wait()

        @pl.when(rmin < UNDERFLOW_BITS)
        def _():
            wm = km_ref.shape[0]
            mcol = lax.broadcasted_iota(jnp.int32, (tq, wm), 1)
            self.block(km_ref[...], vm_ref[...], tri_ref[0:wm, 0:wm], mcol < n_meta)

    def output(self):
        return jnp.concatenate([self.acc_ref[h] for h in range(self.n_heads)], axis=-1)


def _attn_prompt(q, k, v, km, vm, tri, zs, gs, mc, x, wso, wo, *, n_heads):
    b, t, d = q.shape
    tq = ATTN_TILE
    n_sub = ATTN_TILES_PER_STEP
    hd = d // n_heads
    assert km.shape[1] <= META_KEY_PAD
    tile = pl.BlockSpec((1, n_sub * tq, d), lambda i, j: (i, j, 0))
    prev_tile = pl.BlockSpec((1, tq, d), lambda i, j: (i, jnp.maximum(n_sub * j - 1, 0), 0))
    hbm = pl.BlockSpec(memory_space=pl.ANY)

    def const(shape):
        return pl.BlockSpec(shape, lambda i, j: (0,) * len(shape), pipeline_mode=pl.Buffered(1))

    old = pltpu.VMEM((2, tq, d), BF16)
    meta = pltpu.VMEM((META_KEY_PAD, d), BF16)
    return pl.pallas_call(
        functools.partial(_attn_prompt_kernel, n_heads=n_heads),
        grid=(b, t // (n_sub * tq)),
        in_specs=[tile, tile, tile, prev_tile, prev_tile, hbm, hbm,
                  const((1,) + km.shape[1:]), const((1,) + vm.shape[1:]),
                  const(tri.shape), tile, tile, tile, tile, const(wso.shape), const(wo.shape)],
        out_specs=tile,
        out_shape=jax.ShapeDtypeStruct((b, t, d), F32),
        scratch_shapes=[pltpu.VMEM((n_sub, n_heads, tq, hd), F32), pltpu.VMEM((n_sub, n_heads, tq, 1), F32),
                        old, old, pltpu.SemaphoreType.DMA((2, 2)), meta, meta],
        compiler_params=pltpu.CompilerParams(
            dimension_semantics=("arbitrary", "arbitrary"),
            vmem_limit_bytes=V7X_VMEM_LIMIT_BYTES),
        name="attn_prompt",
    )(q, k, v, k, v, k, v, km, vm, tri, zs, gs, mc, x, wso, wo)


def _attn_sample_kernel(q_ref, kn_ref, vn_ref, ck_hbm, cv_hbm, tri_ref, zs_ref, gs_ref, mc_ref, x_ref,
                        wso_ref, wo_ref, y_ref, kfirst, vfirst, first_sems, kbuf, vbuf, sems,
                        acc_ref, r_ref, o_ref, *, n_heads, tk):
    b = pl.program_id(0)
    nb = pl.num_programs(0)
    s_new = kn_ref.shape[1]
    d = kn_ref.shape[2]
    hd = d // n_heads
    lanes = n_heads * s_new
    p_len = ck_hbm.shape[1] // n_heads
    nc = p_len // tk

    query = lax.broadcasted_iota(jnp.int32, (s_new, lanes), 0)
    repeat = (lax.broadcasted_iota(jnp.int32, (s_new, lanes), 1) % s_new == query).astype(BF16)
    q_rep = lax.dot_general(q_ref[0], repeat, (((0,), (0,)), ((), ())), preferred_element_type=F32)
    same_head = (lax.broadcasted_iota(jnp.int32, (d, lanes), 0) // hd
                 == lax.broadcasted_iota(jnp.int32, (d, lanes), 1) // s_new)
    qbd = jnp.where(same_head, q_rep, 0.0).astype(BF16)

    def chunk_copies(bb, c, kdst, vdst, sem, slot):
        r0 = pl.multiple_of((p_len - (c + 1) * tk) * n_heads, V7X_SUBLANES)
        rows = pl.ds(r0, tk * n_heads)
        return (pltpu.make_async_copy(ck_hbm.at[bb, rows, :], kdst.at[slot], sem.at[0, slot]),
                pltpu.make_async_copy(cv_hbm.at[bb, rows, :], vdst.at[slot], sem.at[1, slot]))

    def first_copies(bb):
        return chunk_copies(bb, 0, kfirst, vfirst, first_sems, bb & 1)

    def older_copies(c):
        return chunk_copies(b, c, kbuf, vbuf, sems, c & 1)

    @pl.when(b == 0)
    def _():
        for cp in first_copies(b):
            cp.start()

    @pl.when(b + 1 < nb)
    def _():
        for cp in first_copies(b + 1):
            cp.start()

    def block(kblk, vblk, tri, mask):
        z2 = jnp.dot(kblk, qbd, preferred_element_type=F32)
        sp = _softplus2(z2)
        spm = sp if mask is None else jnp.where(mask, sp, 0.0)
        suf = jnp.dot(tri, spm.astype(BF16), preferred_element_type=F32)
        r = r_ref[...]
        a = jnp.exp2((z2 - sp) - (suf + r))
        if mask is not None:
            a = jnp.where(mask, a, 0.0)
        acc_ref[...] += lax.dot_general(a.astype(BF16), vblk, (((0,), (0,)), ((), ())),
                                        preferred_element_type=F32)
        r_new = r + jnp.sum(spm, axis=0, keepdims=True)
        r_ref[...] = r_new
        return jnp.min(r_new)

    acc_ref[...] = jnp.zeros_like(acc_ref)
    r_ref[...] = jnp.zeros_like(r_ref)
    key = lax.broadcasted_iota(jnp.int32, (s_new, lanes), 0)
    qi = lax.broadcasted_iota(jnp.int32, (s_new, lanes), 1) % s_new
    rmin = block(kn_ref[0], vn_ref[0], tri_ref[0:s_new, 0:s_new], key < qi)

    def gather_heads(buf, slot):
        return jnp.concatenate(
            [buf[slot, pl.ds(h, tk, stride=n_heads), :] for h in range(n_heads)], axis=-1).astype(BF16)

    for cp in first_copies(b):
        cp.wait()
    rmin = lax.cond(
        rmin < UNDERFLOW_BITS,
        lambda: block(gather_heads(kfirst, b & 1), gather_heads(vfirst, b & 1), tri_ref[...], None),
        lambda: rmin)

    def cond(cr):
        c, rmin = cr
        return jnp.logical_and(c < nc, rmin < UNDERFLOW_BITS)

    @pl.when(cond((1, rmin)))
    def _():
        for cp in older_copies(1):
            cp.start()

    def body(cr):
        c, _ = cr
        for cp in older_copies(c):
            cp.wait()

        @pl.when(c + 1 < nc)
        def _():
            for cp in older_copies(c + 1):
                cp.start()

        return c + 1, block(gather_heads(kbuf, c & 1), gather_heads(vbuf, c & 1), tri_ref[...], None)

    c_end, _ = lax.while_loop(cond, body, (jnp.int32(1), rmin))

    @pl.when(jnp.logical_and(c_end > 1, c_end < nc))
    def _():
        for cp in older_copies(c_end):
            cp.wait()

    acc = acc_ref[...]
    o_ref[b] = jnp.concatenate(
        [acc[h * s_new:(h + 1) * s_new, h * hd:(h + 1) * hd] for h in range(n_heads)], axis=-1)

    @pl.when(b == nb - 1)
    def _():
        rows = nb * s_new
        lead = zs_ref.shape[0] - nb
        flat = lambda ref, skip: ref[skip:].reshape(rows, d)
        y = _merge_out(flat(o_ref, 0), flat(zs_ref, lead), flat(gs_ref, lead), flat(mc_ref, lead),
                       flat(x_ref, 0), wso_ref, wo_ref)
        y_ref[...] = y.reshape(nb, s_new, d)


def _attn_sample(q, kn, vn, ck, cv, tri, zs, gs, mc, x, wso, wo, *, n_heads):
    b, s_new, d = x.shape
    lead = kn.shape[0] - b
    hd = d // n_heads
    tk = SAMPLE_KEY_TILE
    lanes = n_heads * s_new
    per_b = lambda shape, skip=0: pl.BlockSpec((1,) + shape, lambda i: (i + skip, 0, 0))
    whole = lambda a: pl.BlockSpec(a.shape, lambda i: (0, 0, 0), pipeline_mode=pl.Buffered(1))
    hbm = pl.BlockSpec(memory_space=pl.ANY)
    chunk = pltpu.VMEM((2, tk * n_heads, hd), F32)

    def const(shape):
        return pl.BlockSpec(shape, lambda i: (0,) * len(shape), pipeline_mode=pl.Buffered(1))

    return pl.pallas_call(
        functools.partial(_attn_sample_kernel, n_heads=n_heads, tk=tk),
        grid=(b,),
        in_specs=[per_b((s_new, d), lead), per_b((s_new, d), lead), per_b((s_new, d), lead), hbm, hbm,
                  const(tri.shape), whole(zs), whole(gs), whole(mc), whole(x),
                  const(wso.shape), const(wo.shape)],
        out_specs=pl.BlockSpec((b, s_new, d), lambda i: (0, 0, 0)),
        out_shape=jax.ShapeDtypeStruct((b, s_new, d), F32),
        scratch_shapes=[chunk, chunk, pltpu.SemaphoreType.DMA((2, 2)),
                        chunk, chunk, pltpu.SemaphoreType.DMA((2, 2)),
                        pltpu.VMEM((lanes, d), F32), pltpu.VMEM((1, lanes), F32),
                        pltpu.VMEM((b, s_new, d), F32)],
        compiler_params=pltpu.CompilerParams(
            dimension_semantics=("arbitrary",),
            vmem_limit_bytes=V7X_VMEM_LIMIT_BYTES),
        name="attn_sample",
    )(q, kn, vn, ck, cv, tri, zs, gs, mc, x, wso, wo)


def kernel(x_prompt, x_sample, cache_k, cache_v, state_conv, meta_tokens, norm_g, w_in, b_in,
           q_norm_g, k_norm_g, conv_w, w_conv_o, w_sb_o, w_o):
    depth, dec_b, p_len, n_heads, hd = cache_k.shape
    assert depth == 1, "single-layer stack only"
    b, t, d = x_prompt.shape
    s_new = x_sample.shape[1]
    n_meta = meta_tokens.shape[0]
    assert n_heads * hd == d and conv_w.shape[1] == 3 and w_in.shape[2] == N_SPLITS * d
    assert n_meta == s_new, "meta tokens and running streams share one projection call"
    assert t % (ATTN_TILE * ATTN_TILES_PER_STEP) == 0 and t % PROMPT_ROW_TILE == 0
    assert p_len % SAMPLE_KEY_TILE == 0
    assert n_meta <= META_KEY_PAD

    small = (norm_g[0][None], b_in[0][None], q_norm_g[0][None], k_norm_g[0][None], conv_w[0])
    ng, bin_, qg, kg, cw = small

    x_ms = jnp.concatenate([meta_tokens[None], x_sample], axis=0)
    prev_ms = jnp.concatenate([jnp.zeros((1, 2, d), F32), state_conv[0]], axis=0)
    ms = _proj(x_ms, prev_ms, ng, w_in[0], bin_, qg, kg, cw, w_conv_o[0], block_b=dec_b + 1, block_t=s_new,
               n_heads=n_heads, merge_w=(w_sb_o[0], w_o[0]))
    wso, wo = ms["wso"], ms["wo"]

    pr = _proj(x_prompt, ms["cs"], ng, ms["win"], bin_, qg, kg, cw, ms["wco"], block_b=1,
               block_t=PROMPT_ROW_TILE, n_heads=n_heads, lead_kv=(ms["k32"], ms["v32"]), shared_prev=True)

    idx = jnp.arange(ATTN_TILE)
    tri_q = (idx[:, None] > idx[None, :]).astype(BF16)
    y_prompt = _attn_prompt(pr["q"], pr["k"], pr["v"], ms["k"], ms["v"], tri_q, pr["zs"], pr["gs"], pr["mc"],
                            x_prompt, wso, wo, n_heads=n_heads)

    idk = jnp.arange(SAMPLE_KEY_TILE)
    tri_k = (idk[None, :] > idk[:, None]).astype(BF16)
    ck = cache_k[0].reshape(dec_b, p_len * n_heads, hd)
    cv = cache_v[0].reshape(dec_b, p_len * n_heads, hd)
    y_sample = _attn_sample(ms["q"], ms["k"], ms["v"], ck, cv, tri_k, ms["zs"], ms["gs"], ms["mc"],
                            x_sample, wso, wo, n_heads=n_heads)

    def heads(a):
        return a.reshape(1, a.shape[0], a.shape[1] // n_heads, n_heads, hd)

    return (y_prompt, y_sample, heads(pr["k32"]), heads(pr["v32"]), pr["cs"][None],
            heads(ms["k32"][1:]), heads(ms["v32"][1:]), ms["cs"][1:][None])
```
